```python
import math
import jax, jax.numpy as jnp
from jax import lax
import numpy as np

D_MODEL = 4096
BATCH = 4
SEQ = 2048
DEPTH = 4
DEC_BATCH = 128
DEC_SEQ = 4
PAST_LEN = 16384
PAGE_SIZE = 128

POOL_WINDOWS = (2, 4, 8, 16)
N_POOL_GROUPS = len(POOL_WINDOWS)
D_POOL = D_MODEL // 4
POOL_GROUP_W = D_POOL // N_POOL_GROUPS
POOL_BUF = max(POOL_WINDOWS) - 1
HEAD_K = 128
HEAD_V = 128
N_DELTA_HEADS = D_MODEL // 256
D_QK = N_DELTA_HEADS * HEAD_K
D_V = N_DELTA_HEADS * HEAD_V
D_QKV = 2 * D_QK + D_V
CONV_W = 4
DELTA_CHUNK = 64
D_FF = -(-8 * D_MODEL // (3 * 256)) * 256
DN_ALPHA = (2 * DEPTH) ** 0.25
DN_BETA = (8 * DEPTH) ** -0.25
LN_EPS = 1e-5
RMS_EPS = 1e-6
OFF_POOL = 0
OFF_QKV = OFF_POOL + D_POOL
OFF_Z = OFF_QKV + D_QKV
OFF_BETA = OFF_Z + D_V
OFF_A = OFF_BETA + N_DELTA_HEADS
OFF_GA = OFF_A + N_DELTA_HEADS
OFF_GB = OFF_GA + D_MODEL
D_IN = OFF_GB + D_MODEL

kernel_name = 'pool_gated_deltanet_deepnorm_step'

F32 = jnp.float32


def layer_norm(x, g, b):
    xf = x.astype(F32)
    mu = jnp.mean(xf, axis=-1, keepdims=True)
    xc = xf - mu
    var = jnp.mean(xc * xc, axis=-1, keepdims=True)
    return (xc * lax.rsqrt(var + LN_EPS) * g.astype(F32) + b.astype(F32)).astype(x.dtype)


def pool_mixer(u, buf, start_pos, w_grp, scale):
    B, T, _ = u.shape
    ext = jnp.concatenate([buf.astype(u.dtype), u], axis=1)
    extf = ext.astype(F32)
    csum = jnp.pad(jnp.cumsum(extf, axis=1), ((0, 0), (1, 0), (0, 0)))
    uf = extf[:, POOL_BUF:]
    pos = start_pos + jnp.arange(T)
    groups = []
    for gi, w in enumerate(POOL_WINDOWS):
        cs = csum[..., gi * POOL_GROUP_W:(gi + 1) * POOL_GROUP_W]
        win = cs[:, POOL_BUF + 1:POOL_BUF + 1 + T] - cs[:, POOL_BUF + 1 - w:POOL_BUF + 1 - w + T]
        cnt = jnp.minimum(pos + 1, w).astype(F32)[None, :, None]
        groups.append(win / cnt - uf[..., gi * POOL_GROUP_W:(gi + 1) * POOL_GROUP_W])
    mixed = jnp.stack(groups, axis=2)
    y = jnp.einsum('btgc,gcd->btgd', mixed, w_grp.astype(F32)).reshape(B, T, D_POOL)
    y = y * scale.astype(F32)
    return y.astype(u.dtype), ext[:, -POOL_BUF:]


def short_conv(u, buf, w):
    T = u.shape[1]
    ext = jnp.concatenate([buf.astype(u.dtype), u], axis=1)
    extf = ext.astype(F32)
    wf = w.astype(F32)
    y = extf[:, 0:T] * wf[0]
    for i in range(1, CONV_W):
        y = y + extf[:, i:i + T] * wf[i]
    return jax.nn.silu(y), ext[:, -(CONV_W - 1):]


def split_heads(a, d):
    B, T, _ = a.shape
    return a.astype(F32).reshape(B, T, -1, d).transpose(0, 2, 1, 3)


def l2norm(a):
    return a * lax.rsqrt(jnp.sum(a * a, axis=-1, keepdims=True) + RMS_EPS)


def delta_chunk(S0, q, k, v, g, beta):
    C = q.shape[-2]
    G = jnp.cumsum(g, axis=-1)
    idx = jnp.arange(C)
    incl = idx[:, None] >= idx[None, :]
    strict = idx[:, None] > idx[None, :]
    decay = jnp.exp(jnp.where(incl, G[..., :, None] - G[..., None, :], -jnp.inf))
    kk = jnp.einsum('bhtd,bhsd->bhts', k, k)
    A = jnp.where(strict, beta[..., :, None] * kk * decay, 0.0)
    eG = jnp.exp(G)[..., None]
    rhs = beta[..., None] * (v - eG * jnp.einsum('bhtd,bhde->bhte', k, S0))
    U = lax.linalg.triangular_solve(A + jnp.eye(C, dtype=F32), rhs, left_side=True,
                                    lower=True, unit_diagonal=True)
    qk = jnp.einsum('bhtd,bhsd->bhts', q, k) * decay
    o = eG * jnp.einsum('bhtd,bhde->bhte', q, S0) + jnp.einsum('bhts,bhse->bhte', qk, U)
    tail = jnp.exp(G[..., -1:] - G)[..., None]
    S_new = jnp.exp(G[..., -1])[..., None, None] * S0 + jnp.einsum('bhtd,bhte->bhde', k * tail, U)
    return S_new, o


def gated_delta_rule(S0, q, k, v, g, beta):
    B, H, T, _ = q.shape
    C = DELTA_CHUNK if T % DELTA_CHUNK == 0 else T
    n = T // C

    def to_chunks(a):
        return jnp.moveaxis(a.reshape((B, H, n, C) + a.shape[3:]), 2, 0)

    xs = (to_chunks(q), to_chunks(k), to_chunks(v), to_chunks(g), to_chunks(beta))
    S_fin, o = lax.scan(lambda S, c: delta_chunk(S, *c), S0, xs)
    o = jnp.moveaxis(o, 0, 2).reshape(B, H, T, HEAD_V)
    return o, S_fin


def trunk_layer(x, pool_buf, conv_buf, S0, start_pos, w_in, w_pool_grp, pool_scale, w_pool_up,
                conv_w, a_log, dt_bias, o_norm_w, w_delta_up, w_out, ln1_g, ln1_b,
                w_gate, w_up, w_down, ln2_g, ln2_b):
    B, T, _ = x.shape
    h = x @ w_in
    u_pool = h[..., OFF_POOL:OFF_QKV]
    qkv = h[..., OFF_QKV:OFF_Z]
    z = h[..., OFF_Z:OFF_BETA]
    b_logit = h[..., OFF_BETA:OFF_A]
    a_in = h[..., OFF_A:OFF_GA]
    gate_a = jax.nn.sigmoid(h[..., OFF_GA:OFF_GB])
    gate_b = jax.nn.sigmoid(h[..., OFF_GB:D_IN])

    y_pool, pool_buf_new = pool_mixer(u_pool, pool_buf, start_pos, w_pool_grp, pool_scale)
    y_a = y_pool @ w_pool_up

    qkv_c, conv_buf_new = short_conv(qkv, conv_buf, conv_w)
    q = l2norm(split_heads(qkv_c[..., :D_QK], HEAD_K)) * (HEAD_K ** -0.5)
    k = l2norm(split_heads(qkv_c[..., D_QK:2 * D_QK], HEAD_K))
    v = split_heads(qkv_c[..., 2 * D_QK:], HEAD_V)
    beta = jax.nn.sigmoid(b_logit.astype(F32)).transpose(0, 2, 1)
    g = -(jnp.exp(a_log.astype(F32)) *
          jax.nn.softplus(a_in.astype(F32) + dt_bias.astype(F32))).transpose(0, 2, 1)
    o, S_new = gated_delta_rule(S0.astype(F32), q, k, v, g, beta)
    o = o.transpose(0, 2, 1, 3)
    o = o * lax.rsqrt(jnp.mean(o * o, axis=-1, keepdims=True) + RMS_EPS) * o_norm_w.astype(F32)
    o = o * jax.nn.silu(z.astype(F32).reshape(B, T, N_DELTA_HEADS, HEAD_V))
    y_b = o.reshape(B, T, D_V).astype(x.dtype) @ w_delta_up

    merged = gate_a * y_a + gate_b * y_b
    x = layer_norm(DN_ALPHA * x + merged @ w_out, ln1_g, ln1_b)
    f = (jax.nn.silu(x @ w_gate) * (x @ w_up)) @ w_down
    x = layer_norm(DN_ALPHA * x + f, ln2_g, ln2_b)
    return x, pool_buf_new, conv_buf_new, S_new.astype(S0.dtype)


def setup_inputs(seed: int = 0) -> dict:
    key = jax.random.key(seed)
    ks = jax.random.split(key, 24)
    nrm = jax.random.normal
    w_in = nrm(ks[0], (DEPTH, D_MODEL, D_IN), F32) * D_MODEL ** -0.5
    v_cols = (jnp.arange(D_IN) >= OFF_QKV + 2 * D_QK) & (jnp.arange(D_IN) < OFF_Z)
    w_in = w_in * jnp.where(v_cols, DN_BETA, 1.0).astype(F32)
    dt = jnp.exp(jax.random.uniform(ks[1], (DEPTH, N_DELTA_HEADS), F32,
                                    math.log(1e-3), math.log(1e-1)))
    return {
        'x_prompt': nrm(ks[2], (BATCH, SEQ, D_MODEL), F32),
        'x_sample': nrm(ks[3], (DEC_BATCH, DEC_SEQ, D_MODEL), F32),
        'state_pool': nrm(ks[4], (DEPTH, DEC_BATCH, POOL_BUF, D_POOL), F32),
        'state_conv': nrm(ks[5], (DEPTH, DEC_BATCH, CONV_W - 1, D_QKV), F32),
        'state_delta': nrm(ks[6], (DEPTH, DEC_BATCH, N_DELTA_HEADS, HEAD_K, HEAD_V), F32) * 0.1,
        'w_in': w_in,
        'w_pool_grp': nrm(ks[7], (DEPTH, N_POOL_GROUPS, POOL_GROUP_W, POOL_GROUP_W), F32) * POOL_GROUP_W ** -0.5,
        'pool_scale': 1.0 + 0.02 * nrm(ks[8], (DEPTH, D_POOL), F32),
        'w_pool_up': nrm(ks[9], (DEPTH, D_POOL, D_MODEL), F32) * (D_POOL ** -0.5 * DN_BETA),
        'conv_w': nrm(ks[10], (DEPTH, CONV_W, D_QKV), F32) * CONV_W ** -0.5,
        'a_log': jnp.log(jax.random.uniform(ks[11], (DEPTH, N_DELTA_HEADS), F32, 1.0, 16.0)),
        'dt_bias': dt + jnp.log(-jnp.expm1(-dt)),
        'o_norm_w': 1.0 + 0.02 * nrm(ks[12], (DEPTH, HEAD_V), F32),
        'w_delta_up': nrm(ks[13], (DEPTH, D_V, D_MODEL), F32) * (D_V ** -0.5 * DN_BETA),
        'w_out': nrm(ks[14], (DEPTH, D_MODEL, D_MODEL), F32) * (D_MODEL ** -0.5 * DN_BETA),
        'ln1_g': 1.0 + 0.02 * nrm(ks[15], (DEPTH, D_MODEL), F32),
        'ln1_b': 0.02 * nrm(ks[16], (DEPTH, D_MODEL), F32),
        'w_gate': nrm(ks[17], (DEPTH, D_MODEL, D_FF), F32) * D_MODEL ** -0.5,
        'w_up': nrm(ks[18], (DEPTH, D_MODEL, D_FF), F32) * D_MODEL ** -0.5,
        'w_down': nrm(ks[19], (DEPTH, D_FF, D_MODEL), F32) * (D_FF ** -0.5 * DN_BETA),
        'ln2_g': 1.0 + 0.02 * nrm(ks[20], (DEPTH, D_MODEL), F32),
        'ln2_b': 0.02 * nrm(ks[21], (DEPTH, D_MODEL), F32),
    }


def reference(x_prompt, x_sample, state_pool, state_conv, state_delta, w_in, w_pool_grp, pool_scale,
              w_pool_up, conv_w, a_log, dt_bias, o_norm_w, w_delta_up, w_out, ln1_g, ln1_b,
              w_gate, w_up, w_down, ln2_g, ln2_b):
    bp = x_prompt.shape[0]
    xp, xs = x_prompt, x_sample
    pool_p, conv_p, delta_p, pool_s, conv_s, delta_s = [], [], [], [], [], []
    for l in range(DEPTH):
        p = (w_in[l], w_pool_grp[l], pool_scale[l], w_pool_up[l], conv_w[l], a_log[l], dt_bias[l],
             o_norm_w[l], w_delta_up[l], w_out[l], ln1_g[l], ln1_b[l], w_gate[l], w_up[l], w_down[l],
             ln2_g[l], ln2_b[l])
        xp, pb, cb, sb = trunk_layer(
            xp, jnp.zeros((bp, POOL_BUF, D_POOL), xp.dtype), jnp.zeros((bp, CONV_W - 1, D_QKV), xp.dtype),
            jnp.zeros((bp, N_DELTA_HEADS, HEAD_K, HEAD_V), state_delta.dtype), 0, *p)
        xs, pb2, cb2, sb2 = trunk_layer(xs, state_pool[l], state_conv[l], state_delta[l], PAST_LEN, *p)
        pool_p.append(pb); conv_p.append(cb); delta_p.append(sb)
        pool_s.append(pb2); conv_s.append(cb2); delta_s.append(sb2)
    return (xp, xs, jnp.stack(pool_p), jnp.stack(conv_p), jnp.stack(delta_p),
            jnp.stack(pool_s), jnp.stack(conv_s), jnp.stack(delta_s))
```

```python
import functools
import math

import jax
import jax.numpy as jnp
from jax import lax
from jax.experimental import pallas as pl
from jax.experimental.pallas import tpu as pltpu

F32 = jnp.float32
BF16 = jnp.bfloat16

POOL_WINDOWS = (2, 4, 8, 16)
DELTA_CHUNK = 64
PAST_LEN = 16384
LN_EPS = 1e-5
RMS_EPS = 1e-6

V7X_LANES = 128
V7X_SUBLANES = 8
V7X_BF16_ROWS = 16
V7X_VMEM_LIMIT_BYTES = 56 * 1024 * 1024
POOL_HALO = 16
CONV_HALO = V7X_SUBLANES


def _params(*semantics):
    return pltpu.CompilerParams(dimension_semantics=semantics, vmem_limit_bytes=V7X_VMEM_LIMIT_BYTES)


def _divisor(n, target, mult):
    best = None
    for d in range(mult, min(n, target) + 1, mult):
        if n % d == 0:
            best = d
    assert best is not None, (n, target, mult)
    return best


def _dot(a, b):
    return jnp.dot(a, b, preferred_element_type=F32)


def _dot_t(a, b):
    return lax.dot_general(a, b, (((1,), (1,)), ((), ())), preferred_element_type=F32)


def _tdot(a, b):
    return lax.dot_general(a, b, (((0,), (0,)), ((), ())), preferred_element_type=F32)


def _softplus(x):
    return jnp.maximum(x, 0.0) + jnp.log1p(jnp.exp(-jnp.abs(x)))


def _inproj_body(x_ref, w_ref, ws_ref, o_ref, os_ref):
    x = x_ref[...]
    o_ref[...] = _dot(x, w_ref[...])

    @pl.when(pl.program_id(1) == 0)
    def _():
        os_ref[...] = _dot(x, ws_ref[...])


def _inproj(x, w_main, w_small, l):
    M, K = x.shape
    N = w_main.shape[-1]
    NS = w_small.shape[-1]
    bm = _divisor(M, 1088, V7X_BF16_ROWS)
    bn = _divisor(N, 1024, V7X_LANES)
    return pl.pallas_call(
        _inproj_body,
        grid=(M // bm, N // bn),
        in_specs=[
            pl.BlockSpec((bm, K), lambda i, j: (i, 0)),
            pl.BlockSpec((None, K, bn), lambda i, j: (l, 0, j)),
            pl.BlockSpec((None, K, NS), lambda i, j: (l, 0, 0)),
        ],
        out_specs=[
            pl.BlockSpec((bm, bn), lambda i, j: (i, j)),
            pl.BlockSpec((bm, NS), lambda i, j: (i, 0)),
        ],
        out_shape=[jax.ShapeDtypeStruct((M, N), F32), jax.ShapeDtypeStruct((M, NS), F32)],
        compiler_params=_params("parallel", "arbitrary"),
        name="inproj",
    )(x, w_main, w_small)


def _merge_body(yp_ref, ogp_ref, ogs_ref, wpu_ref, wdu_ref, ga_ref, gb_ref, o_ref, *, n_prompt_blocks):
    og = jnp.where(pl.program_id(1) < n_prompt_blocks, ogp_ref[...], ogs_ref[...])
    ya = _dot(yp_ref[...], wpu_ref[...])
    yb = _dot(og, wdu_ref[...])
    o_ref[...] = (jax.nn.sigmoid(ga_ref[...]) * ya + jax.nn.sigmoid(gb_ref[...]) * yb).astype(o_ref.dtype)


def _merge(yp, og_p, og_s, w_pool_up, w_delta_up, h_main, l, col_ga, col_gb, tb):
    M, DP = yp.shape
    DV = og_p.shape[1]
    D = w_pool_up.shape[-1]
    bn = _divisor(math.gcd(D, col_ga, col_gb), 1024, V7X_LANES)
    npb = og_p.shape[0] // tb
    assert og_s.shape[0] == tb and col_ga % bn == 0 and col_gb % bn == 0
    return pl.pallas_call(
        functools.partial(_merge_body, n_prompt_blocks=npb),
        grid=(D // bn, M // tb),
        in_specs=[
            pl.BlockSpec((tb, DP), lambda j, i: (i, 0)),
            pl.BlockSpec((tb, DV), lambda j, i: (jnp.minimum(i, npb - 1), 0)),
            pl.BlockSpec((tb, DV), lambda j, i: (0, 0)),
            pl.BlockSpec((None, DP, bn), lambda j, i: (l, 0, j)),
            pl.BlockSpec((None, DV, bn), lambda j, i: (l, 0, j)),
            pl.BlockSpec((tb, bn), lambda j, i: (i, col_ga // bn + j)),
            pl.BlockSpec((tb, bn), lambda j, i: (i, col_gb // bn + j)),
        ],
        out_specs=pl.BlockSpec((tb, bn), lambda j, i: (i, j)),
        out_shape=jax.ShapeDtypeStruct((M, D), BF16),
        compiler_params=_params("parallel", "parallel"),
        name="merge",
    )(yp, og_p, og_s, w_pool_up, w_delta_up, h_main, h_main)


def _outproj_body(a_ref, w_ref, x_ref, o_ref, *, alpha):
    o_ref[...] = alpha * x_ref[...] + _dot(a_ref[...], w_ref[...])


def _outproj(a, w, x, l, alpha):
    M, K = a.shape
    N = w.shape[-1]
    bm = _divisor(M, 1088, V7X_BF16_ROWS)
    bn = _divisor(N, 512, V7X_LANES)
    return pl.pallas_call(
        functools.partial(_outproj_body, alpha=alpha),
        grid=(M // bm, N // bn),
        in_specs=[
            pl.BlockSpec((bm, K), lambda i, j: (i, 0)),
            pl.BlockSpec((None, K, bn), lambda i, j: (l, 0, j)),
            pl.BlockSpec((bm, bn), lambda i, j: (i, j)),
        ],
        out_specs=pl.BlockSpec((bm, bn), lambda i, j: (i, j)),
        out_shape=jax.ShapeDtypeStruct((M, N), F32),
        compiler_params=_params("parallel", "parallel"),
        name="outproj",
    )(a, w, x)


def _glu_body(x_ref, wg_ref, wu_ref, o_ref):
    x = x_ref[...]
    g = _dot(x, wg_ref[...])
    u = _dot(x, wu_ref[...])
    o_ref[...] = (g * jax.nn.sigmoid(g) * u).astype(o_ref.dtype)


def _glu(x, w_gate, w_up, l):
    M, K = x.shape
    N = w_gate.shape[-1]
    bm = _divisor(M, 1088, V7X_BF16_ROWS)
    bn = _divisor(N, 512, V7X_LANES)
    return pl.pallas_call(
        _glu_body,
        grid=(M // bm, N // bn),
        in_specs=[
            pl.BlockSpec((bm, K), lambda i, j: (i, 0)),
            pl.BlockSpec((None, K, bn), lambda i, j: (l, 0, j)),
            pl.BlockSpec((None, K, bn), lambda i, j: (l, 0, j)),
        ],
        out_specs=pl.BlockSpec((bm, bn), lambda i, j: (i, j)),
        out_shape=jax.ShapeDtypeStruct((M, N), BF16),
        compiler_params=_params("parallel", "parallel"),
        name="glu",
    )(x, w_gate, w_up)


def _down_body(a_ref, w_ref, x_ref, o_ref, *, alpha):
    @pl.when(pl.program_id(2) == 0)
    def _():
        o_ref[...] = alpha * x_ref[...]

    o_ref[...] += _dot(a_ref[...], w_ref[...])


def _down(a, w, x, l, alpha):
    M, K = a.shape
    N = w.shape[-1]
    bm = _divisor(M, 1088, V7X_BF16_ROWS)
    bn = _divisor(N, 1024, V7X_LANES)
    tk = _divisor(K, 2816, V7X_LANES)
    return pl.pallas_call(
        functools.partial(_down_body, alpha=alpha),
        grid=(M // bm, N // bn, K // tk),
        in_specs=[
            pl.BlockSpec((bm, tk), lambda i, j, k: (i, k)),
            pl.BlockSpec((None, tk, bn), lambda i, j, k: (l, k, j)),
            pl.BlockSpec((bm, bn), lambda i, j, k: (i, j)),
        ],
        out_specs=pl.BlockSpec((bm, bn), lambda i, j, k: (i, j)),
        out_shape=jax.ShapeDtypeStruct((M, N), F32),
        compiler_params=_params("parallel", "parallel", "arbitrary"),
        name="down",
    )(a, w, x)


def _ln_body(r_ref, g_ref, b_ref, xf_ref, xb_ref):
    r = r_ref[...]
    mu = jnp.mean(r, axis=-1, keepdims=True)
    xc = r - mu
    var = jnp.mean(xc * xc, axis=-1, keepdims=True)
    y = xc * lax.rsqrt(var + LN_EPS) * g_ref[...] + b_ref[...]
    xf_ref[...] = y
    xb_ref[...] = y.astype(xb_ref.dtype)


def _layer_norm(r, g, b, l):
    M, D = r.shape
    bm = _divisor(M, 272, V7X_BF16_ROWS)
    return pl.pallas_call(
        _ln_body,
        grid=(M // bm,),
        in_specs=[
            pl.BlockSpec((bm, D), lambda i: (i, 0)),
            pl.BlockSpec((None, 1, D), lambda i: (l, 0, 0)),
            pl.BlockSpec((None, 1, D), lambda i: (l, 0, 0)),
        ],
        out_specs=[pl.BlockSpec((bm, D), lambda i: (i, 0)), pl.BlockSpec((bm, D), lambda i: (i, 0))],
        out_shape=[jax.ShapeDtypeStruct((M, D), F32), jax.ShapeDtypeStruct((M, D), BF16)],
        compiler_params=_params("parallel"),
        name="layer_norm",
    )(r, g, b)


def _pool_body(u_ref, st_ref, wg_ref, sc_ref, o_ref, ext_ref, *, tb, bs, ts, pb, gw, n_prompt_steps, steps_per_seq):
    step = pl.program_id(0)

    def finish(mixed):
        for gi, m in enumerate(mixed):
            cols = slice(gi * gw, (gi + 1) * gw)
            y = _dot(m.astype(BF16), wg_ref[gi])
            o_ref[:, cols] = (y * sc_ref[:, cols]).astype(o_ref.dtype)

    @pl.when(step < n_prompt_steps)
    def _():
        blk = step % steps_per_seq

        @pl.when(blk == 0)
        def _():
            ext_ref[0:POOL_HALO, :] = jnp.zeros((POOL_HALO, ext_ref.shape[1]), F32)

        ext_ref[POOL_HALO:POOL_HALO + tb, :] = u_ref[...]
        pos = blk * tb + lax.broadcasted_iota(jnp.int32, (tb, 1), 0)
        mixed = []
        for gi, w in enumerate(POOL_WINDOWS):
            cols = slice(gi * gw, (gi + 1) * gw)
            cur = ext_ref[POOL_HALO:POOL_HALO + tb, cols]
            win = cur
            for i in range(1, w):
                win = win + ext_ref[POOL_HALO - i:POOL_HALO - i + tb, cols]
            cnt = jnp.minimum(pos + 1, w).astype(F32)
            mixed.append(win / cnt - cur)
        finish(mixed)
        ext_ref[0:POOL_HALO, :] = ext_ref[tb:tb + POOL_HALO, :]

    @pl.when(step >= n_prompt_steps)
    def _():
        mixed = []
        for gi, w in enumerate(POOL_WINDOWS):
            cols = slice(gi * gw, (gi + 1) * gw)
            per_t = []
            for t in range(ts):
                acc = None
                for i in range(w):
                    e = pb + t - i
                    term = st_ref[e, :, cols] if e < pb else u_ref[(e - pb) * bs:(e - pb + 1) * bs, cols]
                    acc = term if acc is None else acc + term
                cnt = float(min(PAST_LEN + t + 1, w))
                per_t.append(acc / cnt - u_ref[t * bs:(t + 1) * bs, cols])
            mixed.append(jnp.concatenate(per_t, axis=0))
        finish(mixed)


def _pool(h_main, st_pool_tm, w_grp, scale, l, col_pool, tb, bs, ts, n_prompt_steps, steps_per_seq):
    M = h_main.shape[0]
    _, pb, _, DP = st_pool_tm.shape
    G, gw = w_grp.shape[1], w_grp.shape[2]
    assert G == len(POOL_WINDOWS) and pb == max(POOL_WINDOWS) - 1 and pb <= POOL_HALO and col_pool % DP == 0
    body = functools.partial(_pool_body, tb=tb, bs=bs, ts=ts, pb=pb, gw=gw,
                             n_prompt_steps=n_prompt_steps, steps_per_seq=steps_per_seq)
    return pl.pallas_call(
        body,
        grid=(M // tb,),
        in_specs=[
            pl.BlockSpec((tb, DP), lambda s: (s, col_pool // DP)),
            pl.BlockSpec((None, pb, bs, DP), lambda s: (l, 0, 0, 0)),
            pl.BlockSpec((None, G, gw, gw), lambda s: (l, 0, 0, 0)),
            pl.BlockSpec((None, 1, DP), lambda s: (l, 0, 0)),
        ],
        out_specs=pl.BlockSpec((tb, DP), lambda s: (s, 0)),
        out_shape=jax.ShapeDtypeStruct((M, DP), BF16),
        scratch_shapes=[pltpu.VMEM((POOL_HALO + tb, DP), F32)],
        compiler_params=_params("arbitrary"),
        name="pool_mixer",
    )(h_main, st_pool_tm, w_grp, scale)


def _prep_body(x_ref, hs_ref, stc_ref, cw_ref, al_ref, dtb_ref, qkv_ref, gb_ref, ext_ref, halo_ref, *,
               tb, bs, ts, cw, nh, hk, chunk, n_prompt_steps, steps_per_seq):
    step = pl.program_id(0)
    sec = pl.program_id(1)
    half = V7X_LANES // 2

    def emit(y):
        @pl.when(sec < 2)
        def _():
            qscale = jnp.where(sec == 0, hk ** -0.5, 1.0).astype(F32)
            for h in range(nh):
                yh = y[:, h * hk:(h + 1) * hk]
                ss = jnp.sum(yh * yh, axis=-1, keepdims=True)
                qkv_ref[h] = yh * (lax.rsqrt(ss + RMS_EPS) * qscale)

        @pl.when(sec == 2)
        def _():
            for h in range(nh):
                qkv_ref[h] = y[:, h * hk:(h + 1) * hk]

    def emit_gates(gcum, beta):
        for h in range(nh):
            gcol = jnp.broadcast_to(gcum[:, nh + h:nh + h + 1], (tb, half))
            bcol = jnp.broadcast_to(beta[:, h:h + 1], (tb, half))
            gb_ref[h] = jnp.concatenate([gcol, bcol], axis=1)

    def gate_values():
        hs = hs_ref[...]
        beta = jax.nn.sigmoid(hs)
        g = -jnp.exp(al_ref[...]) * _softplus(hs + dtb_ref[...])
        return g, beta

    @pl.when(step < n_prompt_steps)
    def _():
        blk = step % steps_per_seq

        @pl.when(blk == 0)
        def _():
            ext_ref[0:CONV_HALO, :] = jnp.zeros((CONV_HALO, ext_ref.shape[1]), F32)

        @pl.when(blk != 0)
        def _():
            ext_ref[0:CONV_HALO, :] = halo_ref[sec]

        ext_ref[CONV_HALO:CONV_HALO + tb, :] = x_ref[...]
        base = CONV_HALO - (cw - 1)
        y = ext_ref[base:base + tb, :] * cw_ref[0:1, :]
        for i in range(1, cw):
            y = y + ext_ref[base + i:base + i + tb, :] * cw_ref[i:i + 1, :]
        halo_ref[sec] = ext_ref[tb:tb + CONV_HALO, :]
        emit(y * jax.nn.sigmoid(y))

        @pl.when(sec == 0)
        def _():
            g, beta = gate_values()
            t_in = lax.broadcasted_iota(jnp.int32, g.shape, 0) % chunk
            d = 1
            while d < chunk:
                g = g + jnp.where(t_in >= d, pltpu.roll(g, d, 0), 0.0)
                d *= 2
            emit_gates(g, beta)

    @pl.when(step >= n_prompt_steps)
    def _():
        def ext_slab(j):
            return stc_ref[j] if j < cw - 1 else x_ref[(j - (cw - 1)) * bs:(j - (cw - 2)) * bs, :]

        per_t = []
        for t in range(ts):
            y = ext_slab(t) * cw_ref[0:1, :]
            for i in range(1, cw):
                y = y + ext_slab(t + i) * cw_ref[i:i + 1, :]
            per_t.append(y)
        y = jnp.concatenate(per_t, axis=0)
        emit(y * jax.nn.sigmoid(y))

        @pl.when(sec == 0)
        def _():
            g, beta = gate_values()
            slabs = [g[0:bs]]
            for t in range(1, ts):
                slabs.append(slabs[-1] + g[t * bs:(t + 1) * bs])
            emit_gates(jnp.concatenate(slabs, axis=0), beta)


def _prep(h_main, h_small, st_conv_tm, conv_w, alog_row, dtb_row, l, col_qkv, tb, bs, ts, nh, hk,
          n_prompt_steps, steps_per_seq):
    M = h_main.shape[0]
    cw = conv_w.shape[1]
    sw = nh * hk
    assert cw - 1 <= CONV_HALO and col_qkv % sw == 0 and 2 * nh <= V7X_LANES
    body = functools.partial(_prep_body, tb=tb, bs=bs, ts=ts, cw=cw, nh=nh, hk=hk, chunk=DELTA_CHUNK,
                             n_prompt_steps=n_prompt_steps, steps_per_seq=steps_per_seq)
    return pl.pallas_call(
        body,
        grid=(M // tb, 3),
        in_specs=[
            pl.BlockSpec((tb, sw), lambda s, c: (s, col_qkv // sw + c)),
            pl.BlockSpec((tb, V7X_LANES), lambda s, c: (s, 0)),
            pl.BlockSpec((None, cw - 1, bs, sw), lambda s, c: (l, 0, 0, c)),
            pl.BlockSpec((None, cw, sw), lambda s, c: (l, 0, c)),
            pl.BlockSpec((None, 1, V7X_LANES), lambda s, c: (l, 0, 0)),
            pl.BlockSpec((None, 1, V7X_LANES), lambda s, c: (l, 0, 0)),
        ],
        out_specs=[
            pl.BlockSpec((nh, tb, hk), lambda s, c: (c, s, 0)),
            pl.BlockSpec((nh, tb, V7X_LANES), lambda s, c: (0, s, 0)),
        ],
        out_shape=[jax.ShapeDtypeStruct((3 * nh, M, hk), F32), jax.ShapeDtypeStruct((nh, M, V7X_LANES), F32)],
        scratch_shapes=[pltpu.VMEM((CONV_HALO + tb, sw), F32), pltpu.VMEM((3, CONV_HALO, sw), F32)],
        compiler_params=_params("arbitrary", "arbitrary"),
        name="delta_prep",
    )(h_main, h_small, st_conv_tm, conv_w, alog_row, dtb_row)


def _gated_norm(o, z, nw):
    o = o * lax.rsqrt(jnp.mean(o * o, axis=-1, keepdims=True) + RMS_EPS) * nw
    return o * (z * jax.nn.sigmoid(z))


def _delta_prompt_body(q_ref, k_ref, v_ref, gb_ref, z_ref, nw_ref, og_ref, so_ref, s_ref, *, hg, c, hk, hv):
    ci = pl.program_id(2)
    half = V7X_LANES // 2

    @pl.when(ci == 0)
    def _():
        s_ref[...] = jnp.zeros(s_ref.shape, F32)

    ti = lax.broadcasted_iota(jnp.int32, (c, c), 0)
    si = lax.broadcasted_iota(jnp.int32, (c, c), 1)
    incl = ti >= si
    strict = ti > si
    for j in range(hg):
        q = q_ref[j]
        k = k_ref[j]
        v = v_ref[j]
        gbv = gb_ref[j]
        g_col = gbv[:, 0:1]
        b_col = gbv[:, half:half + 1]
        g_rows = gbv[:, 0:c]
        g_cols = jnp.concatenate([gbv, gbv], axis=0).T[0:c, 0:c]
        decay = jnp.where(incl, jnp.exp(jnp.where(incl, g_rows - g_cols, 0.0)), 0.0)
        kb = k * b_col
        k16 = k.astype(BF16)
        a = jnp.where(strict, _dot_t(kb.astype(BF16), k16) * decay, 0.0)
        n = -a
        p = a
        span = 2
        while span < c:
            p16 = p.astype(BF16)
            p = _dot(p16, p16)
            n = n + p + _dot(n.astype(BF16), p.astype(BF16))
            span *= 2
        eg = jnp.exp(g_col)
        rhs = jnp.concatenate([kb * eg, v * b_col], axis=1)
        tr = rhs + _dot(n.astype(BF16), rhs.astype(BF16))
        w = tr[:, 0:hk]
        u0 = tr[:, hk:hk + hv]
        qk = _dot_t(q.astype(BF16), k16) * decay
        g_last = g_col[c - 1:c, :]
        k_tail = k * jnp.exp(g_last - g_col)
        s0 = s_ref[j]
        ps = _dot(jnp.concatenate([w, q * eg], axis=0).astype(BF16), s0.astype(BF16))
        u = u0 - ps[0:c]
        o = ps[c:2 * c] + _dot(qk.astype(BF16), u.astype(BF16))
        s_ref[j] = jnp.exp(g_last) * s0 + _tdot(k_tail.astype(BF16), u.astype(BF16))
        og_ref[:, j * hv:(j + 1) * hv] = _gated_norm(o, z_ref[:, j * hv:(j + 1) * hv], nw_ref[...]).astype(og_ref.dtype)

    @pl.when(ci == pl.num_programs(2) - 1)
    def _():
        so_ref[...] = s_ref[...]


def _delta_prompt(qkv, gb, h_main, nw, l, col_z, nb, t, nh, hk, hv, hg):
    c = DELTA_CHUNK
    assert t % c == 0 and nh % hg == 0 and col_z % (hg * hv) == 0 and hk == V7X_LANES and c <= V7X_LANES // 2
    ncs = t // c
    ngr = nh // hg
    body = functools.partial(_delta_prompt_body, hg=hg, c=c, hk=hk, hv=hv)

    def head_spec(sec):
        return pl.BlockSpec((hg, c, hk), lambda b, g, ci: (sec * ngr + g, b * ncs + ci, 0))

    return pl.pallas_call(
        body,
        grid=(nb, ngr, ncs),
        in_specs=[
            head_spec(0), head_spec(1), head_spec(2),
            pl.BlockSpec((hg, c, V7X_LANES), lambda b, g, ci: (g, b * ncs + ci, 0)),
            pl.BlockSpec((c, hg * hv), lambda b, g, ci: (b * ncs + ci, col_z // (hg * hv) + g)),
            pl.BlockSpec((None, 1, hv), lambda b, g, ci: (l, 0, 0)),
        ],
        out_specs=[
            pl.BlockSpec((c, hg * hv), lambda b, g, ci: (b * ncs + ci, g)),
            pl.BlockSpec((None, hg, hk, hv), lambda b, g, ci: (b, g, 0, 0)),
        ],
        out_shape=[jax.ShapeDtypeStruct((nb * t, nh * hv), BF16), jax.ShapeDtypeStruct((nb, nh, hk, hv), F32)],
        scratch_shapes=[pltpu.VMEM((hg, hk, hv), F32)],
        compiler_params=_params("parallel", "parallel", "arbitrary"),
        name="delta_prompt",
    )(qkv, qkv, qkv, gb, h_main, nw)


def _delta_sample_body(q_ref, k_ref, v_ref, gb_ref, z_ref, nw_ref, s_ref, og_ref, so_ref,
                       kq_scr, p_scr, u_scr, kt_scr, e_scr, *, bs, ts, bb, hk, hv):
    b0 = pl.multiple_of(pl.program_id(1) * bb, bb)
    half = V7X_LANES // 2
    pad = V7X_SUBLANES - ts

    def rows(ref, t):
        return ref[pl.ds(t * bs + b0, bb), :]

    ks = [rows(k_ref, t) for t in range(ts)]
    qs = [rows(q_ref, t) for t in range(ts)]
    for t in range(ts):
        kq_scr[t * bb:(t + 1) * bb, :] = ks[t]
        kq_scr[(ts + t) * bb:(ts + t + 1) * bb, :] = qs[t]

    def state_products(b, carry):
        kq = kq_scr[pl.ds(b, 2 * ts, stride=bb), :]
        p_scr[pl.ds(b, 2 * ts, stride=bb), :] = _dot(kq.astype(BF16), s_ref[b].astype(BF16))
        return carry

    lax.fori_loop(0, bb, state_products, 0)

    gs = [rows(gb_ref, t)[:, 0:1] for t in range(ts)]
    betas = [rows(gb_ref, t)[:, half:half + 1] for t in range(ts)]
    us = []
    for t in range(ts):
        u = betas[t] * (rows(v_ref, t) - jnp.exp(gs[t]) * p_scr[t * bb:(t + 1) * bb, :])
        for s in range(t):
            kk = jnp.sum(ks[t] * ks[s], axis=-1, keepdims=True)
            u = u - (betas[t] * kk * jnp.exp(gs[t] - gs[s])) * us[s]
        us.append(u)
    for t in range(ts):
        o = jnp.exp(gs[t]) * p_scr[(ts + t) * bb:(ts + t + 1) * bb, :]
        for s in range(t + 1):
            qk = jnp.sum(qs[t] * ks[s], axis=-1, keepdims=True)
            o = o + (qk * jnp.exp(gs[t] - gs[s])) * us[s]
        og_ref[pl.ds(t * bs + b0, bb), :] = _gated_norm(o, rows(z_ref, t), nw_ref[...]).astype(og_ref.dtype)
        u_scr[t * bb:(t + 1) * bb, :] = us[t]
        kt_scr[t * bb:(t + 1) * bb, :] = ks[t] * jnp.exp(gs[ts - 1] - gs[t])
    u_scr[ts * bb:(ts + pad) * bb, :] = jnp.zeros((pad * bb, hv), F32)
    kt_scr[ts * bb:(ts + pad) * bb, :] = jnp.zeros((pad * bb, hk), F32)
    e_scr[...] = jnp.broadcast_to(jnp.exp(gs[ts - 1]), (bb, hv))

    def state_update(b, carry):
        ub = u_scr[pl.ds(b, V7X_SUBLANES, stride=bb), :]
        kb = kt_scr[pl.ds(b, V7X_SUBLANES, stride=bb), :]
        so_ref[b] = e_scr[pl.ds(b, 1), :] * s_ref[b] + _tdot(kb.astype(BF16), ub.astype(BF16))
        return carry

    lax.fori_loop(0, bb, state_update, 0)


def _delta_sample(qkv, gb, h_main, nw, state, l, col_z, row0, bs, ts, nh, hk, hv):
    tbs = bs * ts
    bb = _divisor(bs, 64, V7X_BF16_ROWS)
    assert row0 % tbs == 0 and col_z % hv == 0 and 2 * ts <= V7X_SUBLANES
    body = functools.partial(_delta_sample_body, bs=bs, ts=ts, bb=bb, hk=hk, hv=hv)

    def head_spec(sec):
        return pl.BlockSpec((None, tbs, hk), lambda h, g: (sec * nh + h, row0 // tbs, 0))

    return pl.pallas_call(
        body,
        grid=(nh, bs // bb),
        in_specs=[
            head_spec(0), head_spec(1), head_spec(2),
            pl.BlockSpec((None, tbs, V7X_LANES), lambda h, g: (h, row0 // tbs, 0)),
            pl.BlockSpec((tbs, hv), lambda h, g: (row0 // tbs, col_z // hv + h)),
            pl.BlockSpec((None, 1, hv), lambda h, g: (l, 0, 0)),
            pl.BlockSpec((None, bb, None, hk, hv), lambda h, g: (l, g, h, 0, 0)),
        ],
        out_specs=[
            pl.BlockSpec((tbs, hv), lambda h, g: (0, h)),
            pl.BlockSpec((bb, None, hk, hv), lambda h, g: (g, h, 0, 0)),
        ],
        out_shape=[jax.ShapeDtypeStruct((tbs, nh * hv), BF16), jax.ShapeDtypeStruct((bs, nh, hk, hv), F32)],
        scratch_shapes=[
            pltpu.VMEM((2 * ts * bb, hk), F32),
            pltpu.VMEM((2 * ts * bb, hv), F32),
            pltpu.VMEM((V7X_SUBLANES * bb, hv), F32),
            pltpu.VMEM((V7X_SUBLANES * bb, hk), F32),
            pltpu.VMEM((bb, hv), F32),
        ],
        compiler_params=_params("parallel", "arbitrary"),
        name="delta_sample",
    )(qkv, qkv, qkv, gb, h_main, nw, state)


def kernel(x_prompt, x_sample, state_pool, state_conv, state_delta, w_in, w_pool_grp, pool_scale, w_pool_up,
           conv_w, a_log, dt_bias, o_norm_w, w_delta_up, w_out, ln1_g, ln1_b, w_gate, w_up, w_down, ln2_g, ln2_b):
    nb, t, d = x_prompt.shape
    bs, ts, _ = x_sample.shape
    depth = w_in.shape[0]
    pb, dp = state_pool.shape[2], state_pool.shape[3]
    dqkv = state_conv.shape[3]
    nh, hk, hv = state_delta.shape[2], state_delta.shape[3], state_delta.shape[4]
    dv = nh * hv
    dff = w_gate.shape[-1]
    assert dqkv == 2 * nh * hk + dv and hk == hv
    mp, ms = nb * t, bs * ts
    m = mp + ms
    tb = ms
    assert t % tb == 0 and tb % DELTA_CHUNK == 0
    steps_per_seq = t // tb
    n_prompt_steps = mp // tb
    alpha = float((2 * depth) ** 0.25)

    off_qkv = dp
    off_z = off_qkv + dqkv
    off_beta = off_z + dv
    off_ga = off_beta + 2 * nh
    off_gb = off_ga + d
    assert w_in.shape[-1] == off_gb + d
    col_qkv, col_z, col_pool = 0, dqkv, dqkv + dv
    col_ga = col_pool + dp
    col_gb = col_ga + d
    w_main = jnp.concatenate(
        [w_in[..., off_qkv:off_z], w_in[..., off_z:off_beta], w_in[..., 0:dp], w_in[..., off_ga:]],
        axis=-1).astype(BF16)
    w_small = jnp.pad(w_in[..., off_beta:off_ga], ((0, 0), (0, 0), (0, V7X_LANES - 2 * nh))).astype(BF16)
    dffp = -(-dff // 512) * 512
    w_gate_p = jnp.pad(w_gate, ((0, 0), (0, 0), (0, dffp - dff))).astype(BF16)
    w_up_p = jnp.pad(w_up, ((0, 0), (0, 0), (0, dffp - dff))).astype(BF16)
    w_down_p = jnp.pad(w_down, ((0, 0), (0, dffp - dff), (0, 0))).astype(BF16)
    w_grp16 = w_pool_grp.astype(BF16)
    w_pu16 = w_pool_up.astype(BF16)
    w_du16 = w_delta_up.astype(BF16)
    w_out16 = w_out.astype(BF16)

    lane_pad = ((0, 0), (nh, V7X_LANES - 2 * nh))
    alog_row = jnp.pad(a_log, lane_pad)[:, None, :]
    dtb_row = jnp.pad(dt_bias, lane_pad)[:, None, :]
    scale_row = pool_scale[:, None, :]
    nw_row = o_norm_w[:, None, :]
    ln1_g, ln1_b, ln2_g, ln2_b = (a[:, None, :] for a in (ln1_g, ln1_b, ln2_g, ln2_b))
    st_pool_tm = jnp.swapaxes(state_pool, 1, 2)
    st_conv_tm = jnp.swapaxes(state_conv, 1, 2)

    x = jnp.concatenate([x_prompt.reshape(mp, d), jnp.swapaxes(x_sample, 0, 1).reshape(ms, d)], axis=0)
    x16 = x.astype(BF16)

    pool_p, conv_p, delta_p, pool_s, conv_s, delta_s = [], [], [], [], [], []
    for l in range(depth):
        h_main, h_small = _inproj(x16, w_main, w_small, l)
        yp = _pool(h_main, st_pool_tm, w_grp16, scale_row, l, col_pool, tb, bs, ts, n_prompt_steps, steps_per_seq)
        qkv, gb = _prep(h_main, h_small, st_conv_tm, conv_w, alog_row, dtb_row, l, col_qkv, tb, bs, ts, nh, hk,
                        n_prompt_steps, steps_per_seq)
        og_p, s_p = _delta_prompt(qkv, gb, h_main, nw_row, l, col_z, nb, t, nh, hk, hv, hg=4)
        og_s, s_s = _delta_sample(qkv, gb, h_main, nw_row, state_delta, l, col_z, mp, bs, ts, nh, hk, hv)
        merged = _merge(yp, og_p, og_s, w_pu16, w_du16, h_main, l, col_ga, col_gb, tb)
        r1 = _outproj(merged, w_out16, x, l, alpha)
        x1, x1_16 = _layer_norm(r1, ln1_g, ln1_b, l)
        act = _glu(x1_16, w_gate_p, w_up_p, l)
        r2 = _down(act, w_down_p, x1, l, alpha)
        x, x16 = _layer_norm(r2, ln2_g, ln2_b, l)

        hp = h_main[:mp].reshape(nb, t, -1)
        hs = jnp.swapaxes(h_main[mp:].reshape(ts, bs, -1), 0, 1)
        pool_p.append(hp[:, t - pb:, col_pool:col_pool + dp])
        conv_p.append(hp[:, t - (conv_w.shape[1] - 1):, col_qkv:col_qkv + dqkv])
        pool_s.append(jnp.concatenate([state_pool[l], hs[:, :, col_pool:col_pool + dp]], axis=1)[:, -pb:])
        conv_s.append(jnp.concatenate([state_conv[l], hs[:, :, col_qkv:col_qkv + dqkv]], axis=1)[:, -(conv_w.shape[1] - 1):])
        delta_p.append(s_p)
        delta_s.append(s_s)

    y_prompt = x[:mp].reshape(nb, t, d)
    y_sample = jnp.swapaxes(x[mp:].reshape(ts, bs, d), 0, 1)
    return (y_prompt, y_sample, jnp.stack(pool_p), jnp.stack(conv_p), jnp.stack(delta_p),
            jnp.stack(pool_s), jnp.stack(conv_s), jnp.stack(delta_s))
```

```python
import functools
import math

import jax
import jax.numpy as jnp
from jax import lax
from jax.experimental import pallas as pl
from jax.experimental.pallas import tpu as pltpu

F32 = jnp.float32
BF16 = jnp.bfloat16

POOL_WINDOWS = (2, 4, 8, 16)
DELTA_CHUNK = 64
PAST_LEN = 16384
LN_EPS = 1e-5
RMS_EPS = 1e-6

V7X_LANES = 128
V7X_SUBLANES = 8
V7X_BF16_ROWS = 16
V7X_VMEM_LIMIT_BYTES = 56 * 1024 * 1024
POOL_HALO = 16
CONV_HALO = V7X_SUBLANES
GLU_BLOCK_N = 512
SAMPLE_GROUP = 8


def _params(*semantics):
    return pltpu.CompilerParams(dimension_semantics=semantics, vmem_limit_bytes=V7X_VMEM_LIMIT_BYTES)


def _divisor(n, target, mult):
    best = None
    for d in range(mult, min(n, target) + 1, mult):
        if n % d == 0:
            best = d
    assert best is not None, (n, target, mult)
    return best


def _dot(a, b):
    return jnp.dot(a, b, preferred_element_type=F32)


def _dot_t(a, b):
    return lax.dot_general(a, b, (((1,), (1,)), ((), ())), preferred_element_type=F32)


def _tdot(a, b):
    return lax.dot_general(a, b, (((0,), (0,)), ((), ())), preferred_element_type=F32)


def _softplus(x):
    return jnp.maximum(x, 0.0) + jnp.log1p(jnp.exp(-jnp.abs(x)))


def _inproj_body(x_ref, w_ref, ws_ref, o_ref, os_ref):
    x = x_ref[...]
    o_ref[...] = _dot(x, w_ref[...])

    @pl.when(pl.program_id(1) == 0)
    def _():
        os_ref[...] = _dot(x, ws_ref[...])


def _inproj(x, w_main, w_small, l):
    M, K = x.shape
    N = w_main.shape[-1]
    NS = w_small.shape[-1]
    bm = _divisor(M, 1088, V7X_BF16_ROWS)
    bn = _divisor(N, 1024, V7X_LANES)
    return pl.pallas_call(
        _inproj_body,
        grid=(M // bm, N // bn),
        in_specs=[
            pl.BlockSpec((bm, K), lambda i, j: (i, 0)),
            pl.BlockSpec((None, K, bn), lambda i, j: (l, 0, j)),
            pl.BlockSpec((None, K, NS), lambda i, j: (l, 0, 0)),
        ],
        out_specs=[
            pl.BlockSpec((bm, bn), lambda i, j: (i, j)),
            pl.BlockSpec((bm, NS), lambda i, j: (i, 0)),
        ],
        out_shape=[jax.ShapeDtypeStruct((M, N), F32), jax.ShapeDtypeStruct((M, NS), F32)],
        compiler_params=_params("parallel", "arbitrary"),
        name="inproj",
    )(x, w_main, w_small)


def _merge_body(yp_ref, ogp_ref, ogs_ref, wpu_ref, wdu_ref, ga_ref, gb_ref, o_ref, *, n_prompt_blocks):
    og = jnp.where(pl.program_id(1) < n_prompt_blocks, ogp_ref[...], ogs_ref[...])
    ya = _dot(yp_ref[...], wpu_ref[...])
    yb = _dot(og, wdu_ref[...])
    o_ref[...] = (jax.nn.sigmoid(ga_ref[...]) * ya + jax.nn.sigmoid(gb_ref[...]) * yb).astype(o_ref.dtype)


def _merge(yp, og_p, og_s, w_pool_up, w_delta_up, h_main, l, col_ga, col_gb, tb):
    M, DP = yp.shape
    DV = og_p.shape[1]
    D = w_pool_up.shape[-1]
    bn = _divisor(math.gcd(D, col_ga, col_gb), 1024, V7X_LANES)
    npb = og_p.shape[0] // tb
    assert og_s.shape[0] == tb and col_ga % bn == 0 and col_gb % bn == 0
    return pl.pallas_call(
        functools.partial(_merge_body, n_prompt_blocks=npb),
        grid=(D // bn, M // tb),
        in_specs=[
            pl.BlockSpec((tb, DP), lambda j, i: (i, 0)),
            pl.BlockSpec((tb, DV), lambda j, i: (jnp.minimum(i, npb - 1), 0)),
            pl.BlockSpec((tb, DV), lambda j, i: (0, 0)),
            pl.BlockSpec((None, DP, bn), lambda j, i: (l, 0, j)),
            pl.BlockSpec((None, DV, bn), lambda j, i: (l, 0, j)),
            pl.BlockSpec((tb, bn), lambda j, i: (i, col_ga // bn + j)),
            pl.BlockSpec((tb, bn), lambda j, i: (i, col_gb // bn + j)),
        ],
        out_specs=pl.BlockSpec((tb, bn), lambda j, i: (i, j)),
        out_shape=jax.ShapeDtypeStruct((M, D), BF16),
        compiler_params=_params("parallel", "parallel"),
        name="merge",
    )(yp, og_p, og_s, w_pool_up, w_delta_up, h_main, h_main)


def _outproj_body(a_ref, w_ref, x_ref, o_ref, *, alpha):
    o_ref[...] = alpha * x_ref[...] + _dot(a_ref[...], w_ref[...])


def _outproj(a, w, x, l, alpha):
    M, K = a.shape
    N = w.shape[-1]
    bm = _divisor(M, 1088, V7X_BF16_ROWS)
    bn = _divisor(N, 512, V7X_LANES)
    return pl.pallas_call(
        functools.partial(_outproj_body, alpha=alpha),
        grid=(M // bm, N // bn),
        in_specs=[
            pl.BlockSpec((bm, K), lambda i, j: (i, 0)),
            pl.BlockSpec((None, K, bn), lambda i, j: (l, 0, j)),
            pl.BlockSpec((bm, bn), lambda i, j: (i, j)),
        ],
        out_specs=pl.BlockSpec((bm, bn), lambda i, j: (i, j)),
        out_shape=jax.ShapeDtypeStruct((M, N), F32),
        compiler_params=_params("parallel", "parallel"),
        name="outproj",
    )(a, w, x)


def _glu_body(x_ref, wg_ref, wu_ref, o_ref):
    x = x_ref[...]
    g = _dot(x, wg_ref[...])
    u = _dot(x, wu_ref[...])
    o_ref[...] = (g * jax.nn.sigmoid(g) * u).astype(o_ref.dtype)


def _glu(x, w_gate, w_up, l):
    M, K = x.shape
    N = w_gate.shape[-1]
    bm = _divisor(M, 1088, V7X_BF16_ROWS)
    bn = min(N, GLU_BLOCK_N)
    return pl.pallas_call(
        _glu_body,
        grid=(M // bm, pl.cdiv(N, bn)),
        in_specs=[
            pl.BlockSpec((bm, K), lambda i, j: (i, 0)),
            pl.BlockSpec((None, K, bn), lambda i, j: (l, 0, j)),
            pl.BlockSpec((None, K, bn), lambda i, j: (l, 0, j)),
        ],
        out_specs=pl.BlockSpec((bm, bn), lambda i, j: (i, j)),
        out_shape=jax.ShapeDtypeStruct((M, N), BF16),
        compiler_params=_params("parallel", "parallel"),
        name="glu",
    )(x, w_gate, w_up)


def _down_body(a_ref, w_ref, x_ref, o_ref, *, alpha):
    @pl.when(pl.program_id(2) == 0)
    def _():
        o_ref[...] = alpha * x_ref[...]

    o_ref[...] += _dot(a_ref[...], w_ref[...])


def _down(a, w, x, l, alpha):
    M, K = a.shape
    N = w.shape[-1]
    bm = _divisor(M, 1088, V7X_BF16_ROWS)
    bn = _divisor(N, 512, V7X_LANES)
    tk = _divisor(K, 5504, V7X_LANES)
    return pl.pallas_call(
        functools.partial(_down_body, alpha=alpha),
        grid=(M // bm, N // bn, K // tk),
        in_specs=[
            pl.BlockSpec((bm, tk), lambda i, j, k: (i, k)),
            pl.BlockSpec((None, tk, bn), lambda i, j, k: (l, k, j)),
            pl.BlockSpec((bm, bn), lambda i, j, k: (i, j)),
        ],
        out_specs=pl.BlockSpec((bm, bn), lambda i, j, k: (i, j)),
        out_shape=jax.ShapeDtypeStruct((M, N), F32),
        compiler_params=_params("parallel", "parallel", "arbitrary"),
        name="down",
    )(a, w, x)


def _ln_body(r_ref, g_ref, b_ref, xf_ref, xb_ref):
    r = r_ref[...]
    mu = jnp.mean(r, axis=-1, keepdims=True)
    xc = r - mu
    var = jnp.mean(xc * xc, axis=-1, keepdims=True)
    y = xc * lax.rsqrt(var + LN_EPS) * g_ref[...] + b_ref[...]
    xf_ref[...] = y
    xb_ref[...] = y.astype(xb_ref.dtype)


def _layer_norm(r, g, b, l):
    M, D = r.shape
    bm = _divisor(M, 272, V7X_BF16_ROWS)
    return pl.pallas_call(
        _ln_body,
        grid=(M // bm,),
        in_specs=[
            pl.BlockSpec((bm, D), lambda i: (i, 0)),
            pl.BlockSpec((None, 1, D), lambda i: (l, 0, 0)),
            pl.BlockSpec((None, 1, D), lambda i: (l, 0, 0)),
        ],
        out_specs=[pl.BlockSpec((bm, D), lambda i: (i, 0)), pl.BlockSpec((bm, D), lambda i: (i, 0))],
        out_shape=[jax.ShapeDtypeStruct((M, D), F32), jax.ShapeDtypeStruct((M, D), BF16)],
        compiler_params=_params("parallel"),
        name="layer_norm",
    )(r, g, b)


def _pool_body(u_ref, st_ref, wg_ref, sc_ref, o_ref, ext_ref, *, tb, bs, ts, pb, gw, n_prompt_steps, steps_per_seq):
    step = pl.program_id(0)

    def finish(mixed):
        for gi, m in enumerate(mixed):
            cols = slice(gi * gw, (gi + 1) * gw)
            y = _dot(m.astype(BF16), wg_ref[gi])
            o_ref[:, cols] = (y * sc_ref[:, cols]).astype(o_ref.dtype)

    @pl.when(step < n_prompt_steps)
    def _():
        blk = step % steps_per_seq

        @pl.when(blk == 0)
        def _():
            ext_ref[0:POOL_HALO, :] = jnp.zeros((POOL_HALO, ext_ref.shape[1]), F32)

        ext_ref[POOL_HALO:POOL_HALO + tb, :] = u_ref[...]
        pos = blk * tb + lax.broadcasted_iota(jnp.int32, (tb, 1), 0)
        mixed = []
        for gi, w in enumerate(POOL_WINDOWS):
            cols = slice(gi * gw, (gi + 1) * gw)
            cur = ext_ref[POOL_HALO:POOL_HALO + tb, cols]
            win = cur
            for i in range(1, w):
                win = win + ext_ref[POOL_HALO - i:POOL_HALO - i + tb, cols]
            cnt = jnp.minimum(pos + 1, w).astype(F32)
            mixed.append(win / cnt - cur)
        finish(mixed)
        ext_ref[0:POOL_HALO, :] = ext_ref[tb:tb + POOL_HALO, :]

    @pl.when(step >= n_prompt_steps)
    def _():
        mixed = []
        for gi, w in enumerate(POOL_WINDOWS):
            cols = slice(gi * gw, (gi + 1) * gw)
            per_t = []
            for t in range(ts):
                acc = None
                for i in range(w):
                    e = pb + t - i
                    term = st_ref[e, :, cols] if e < pb else u_ref[(e - pb) * bs:(e - pb + 1) * bs, cols]
                    acc = term if acc is None else acc + term
                cnt = float(min(PAST_LEN + t + 1, w))
                per_t.append(acc / cnt - u_ref[t * bs:(t + 1) * bs, cols])
            mixed.append(jnp.concatenate(per_t, axis=0))
        finish(mixed)


def _pool(h_main, st_pool_tm, w_grp, scale, l, col_pool, tb, bs, ts, n_prompt_steps, steps_per_seq):
    M = h_main.shape[0]
    _, pb, _, DP = st_pool_tm.shape
    G, gw = w_grp.shape[1], w_grp.shape[2]
    assert G == len(POOL_WINDOWS) and pb == max(POOL_WINDOWS) - 1 and pb <= POOL_HALO and col_pool % DP == 0
    body = functools.partial(_pool_body, tb=tb, bs=bs, ts=ts, pb=pb, gw=gw,
                             n_prompt_steps=n_prompt_steps, steps_per_seq=steps_per_seq)
    return pl.pallas_call(
        body,
        grid=(M // tb,),
        in_specs=[
            pl.BlockSpec((tb, DP), lambda s: (s, col_pool // DP)),
            pl.BlockSpec((None, pb, bs, DP), lambda s: (l, 0, 0, 0)),
            pl.BlockSpec((None, G, gw, gw), lambda s: (l, 0, 0, 0)),
            pl.BlockSpec((None, 1, DP), lambda s: (l, 0, 0)),
        ],
        out_specs=pl.BlockSpec((tb, DP), lambda s: (s, 0)),
        out_shape=jax.ShapeDtypeStruct((M, DP), BF16),
        scratch_shapes=[pltpu.VMEM((POOL_HALO + tb, DP), F32)],
        compiler_params=_params("arbitrary"),
        name="pool_mixer",
    )(h_main, st_pool_tm, w_grp, scale)


def _prep_body(x_ref, hs_ref, stc_ref, cw_ref, al_ref, dtb_ref, qkv_ref, gb_ref, ext_ref, halo_ref, *,
               tb, bs, ts, cw, nh, hk, chunk, n_prompt_steps, steps_per_seq):
    step = pl.program_id(0)
    sec = pl.program_id(1)
    half = V7X_LANES // 2

    def emit(y):
        @pl.when(sec < 2)
        def _():
            qscale = jnp.where(sec == 0, hk ** -0.5, 1.0).astype(F32)
            for h in range(nh):
                yh = y[:, h * hk:(h + 1) * hk]
                ss = jnp.sum(yh * yh, axis=-1, keepdims=True)
                qkv_ref[h] = yh * (lax.rsqrt(ss + RMS_EPS) * qscale)

        @pl.when(sec == 2)
        def _():
            for h in range(nh):
                qkv_ref[h] = y[:, h * hk:(h + 1) * hk]

    def emit_gates(gcum, beta):
        for h in range(nh):
            gcol = jnp.broadcast_to(gcum[:, nh + h:nh + h + 1], (tb, half))
            bcol = jnp.broadcast_to(beta[:, h:h + 1], (tb, half))
            gb_ref[h] = jnp.concatenate([gcol, bcol], axis=1)

    def gate_values():
        hs = hs_ref[...]
        beta = jax.nn.sigmoid(hs)
        g = -jnp.exp(al_ref[...]) * _softplus(hs + dtb_ref[...])
        return g, beta

    @pl.when(step < n_prompt_steps)
    def _():
        blk = step % steps_per_seq

        @pl.when(blk == 0)
        def _():
            ext_ref[0:CONV_HALO, :] = jnp.zeros((CONV_HALO, ext_ref.shape[1]), F32)

        @pl.when(blk != 0)
        def _():
            ext_ref[0:CONV_HALO, :] = halo_ref[sec]

        ext_ref[CONV_HALO:CONV_HALO + tb, :] = x_ref[...]
        base = CONV_HALO - (cw - 1)
        y = ext_ref[base:base + tb, :] * cw_ref[0:1, :]
        for i in range(1, cw):
            y = y + ext_ref[base + i:base + i + tb, :] * cw_ref[i:i + 1, :]
        halo_ref[sec] = ext_ref[tb:tb + CONV_HALO, :]
        emit(y * jax.nn.sigmoid(y))

        @pl.when(sec == 0)
        def _():
            g, beta = gate_values()
            t_in = lax.broadcasted_iota(jnp.int32, g.shape, 0) % chunk
            d = 1
            while d < chunk:
                g = g + jnp.where(t_in >= d, pltpu.roll(g, d, 0), 0.0)
                d *= 2
            emit_gates(g, beta)

    @pl.when(step >= n_prompt_steps)
    def _():
        def ext_slab(j):
            return stc_ref[j] if j < cw - 1 else x_ref[(j - (cw - 1)) * bs:(j - (cw - 2)) * bs, :]

        per_t = []
        for t in range(ts):
            y = ext_slab(t) * cw_ref[0:1, :]
            for i in range(1, cw):
                y = y + ext_slab(t + i) * cw_ref[i:i + 1, :]
            per_t.append(y)
        y = jnp.concatenate(per_t, axis=0)
        emit(y * jax.nn.sigmoid(y))

        @pl.when(sec == 0)
        def _():
            g, beta = gate_values()
            slabs = [g[0:bs]]
            for t in range(1, ts):
                slabs.append(slabs[-1] + g[t * bs:(t + 1) * bs])
            emit_gates(jnp.concatenate(slabs, axis=0), beta)


def _prep(h_main, h_small, st_conv_tm, conv_w, alog_row, dtb_row, l, col_qkv, tb, bs, ts, nh, hk,
          n_prompt_steps, steps_per_seq):
    M = h_main.shape[0]
    cw = conv_w.shape[1]
    sw = nh * hk
    assert cw - 1 <= CONV_HALO and col_qkv % sw == 0 and 2 * nh <= V7X_LANES
    body = functools.partial(_prep_body, tb=tb, bs=bs, ts=ts, cw=cw, nh=nh, hk=hk, chunk=DELTA_CHUNK,
                             n_prompt_steps=n_prompt_steps, steps_per_seq=steps_per_seq)
    return pl.pallas_call(
        body,
        grid=(M // tb, 3),
        in_specs=[
            pl.BlockSpec((tb, sw), lambda s, c: (s, col_qkv // sw + c)),
            pl.BlockSpec((tb, V7X_LANES), lambda s, c: (s, 0)),
            pl.BlockSpec((None, cw - 1, bs, sw), lambda s, c: (l, 0, 0, c)),
            pl.BlockSpec((None, cw, sw), lambda s, c: (l, 0, c)),
            pl.BlockSpec((None, 1, V7X_LANES), lambda s, c: (l, 0, 0)),
            pl.BlockSpec((None, 1, V7X_LANES), lambda s, c: (l, 0, 0)),
        ],
        out_specs=[
            pl.BlockSpec((nh, tb, hk), lambda s, c: (c, s, 0)),
            pl.BlockSpec((nh, tb, V7X_LANES), lambda s, c: (0, s, 0)),
        ],
        out_shape=[jax.ShapeDtypeStruct((3 * nh, M, hk), F32), jax.ShapeDtypeStruct((nh, M, V7X_LANES), F32)],
        scratch_shapes=[pltpu.VMEM((CONV_HALO + tb, sw), F32), pltpu.VMEM((3, CONV_HALO, sw), F32)],
        compiler_params=_params("arbitrary", "arbitrary"),
        name="delta_prep",
    )(h_main, h_small, st_conv_tm, conv_w, alog_row, dtb_row)


def _gated_norm(o, z, nw):
    o = o * lax.rsqrt(jnp.mean(o * o, axis=-1, keepdims=True) + RMS_EPS) * nw
    return o * (z * jax.nn.sigmoid(z))


def _delta_prompt_body(q_ref, k_ref, v_ref, gb_ref, z_ref, nw_ref, og_ref, so_ref, s_ref, *, hg, c, hk, hv):
    ci = pl.program_id(2)
    half = V7X_LANES // 2

    @pl.when(ci == 0)
    def _():
        s_ref[...] = jnp.zeros(s_ref.shape, F32)

    ti = lax.broadcasted_iota(jnp.int32, (c, c), 0)
    si = lax.broadcasted_iota(jnp.int32, (c, c), 1)
    incl = ti >= si
    strict = ti > si
    heads = range(hg)
    q = [q_ref[j] for j in heads]
    k = [k_ref[j] for j in heads]
    gbv = [gb_ref[j] for j in heads]
    g_col = [x[:, 0:1] for x in gbv]
    b_col = [x[:, half:half + 1] for x in gbv]
    decay = [jnp.where(incl, jnp.exp(jnp.where(incl, x[:, 0:c] - jnp.concatenate([x, x], axis=0).T[0:c, 0:c], 0.0)), 0.0)
             for x in gbv]
    kb = [k[j] * b_col[j] for j in heads]
    k16 = [x.astype(BF16) for x in k]
    a = [jnp.where(strict, _dot_t(kb[j].astype(BF16), k16[j]) * decay[j], 0.0) for j in heads]
    qk = [_dot_t(q[j].astype(BF16), k16[j]) * decay[j] for j in heads]
    n = [-x for x in a]
    p = [_dot(x.astype(BF16), x.astype(BF16)) for x in a]
    span = 4
    while span < c:
        both = [_dot(jnp.concatenate([n[j], p[j]], axis=0).astype(BF16), p[j].astype(BF16)) for j in heads]
        n = [n[j] + p[j] + both[j][0:c] for j in heads]
        p = [both[j][c:2 * c] for j in heads]
        span *= 2
    n = [n[j] + p[j] + _dot(n[j].astype(BF16), p[j].astype(BF16)) for j in heads]
    eg = [jnp.exp(x) for x in g_col]
    rhs = [jnp.concatenate([kb[j] * eg[j], v_ref[j] * b_col[j]], axis=1) for j in heads]
    tr = [rhs[j] + _dot(n[j].astype(BF16), rhs[j].astype(BF16)) for j in heads]
    g_last = [x[c - 1:c, :] for x in g_col]
    k_tail = [(k[j] * jnp.exp(g_last[j] - g_col[j])).astype(BF16) for j in heads]
    s0 = [s_ref[j] for j in heads]
    ps = [_dot(jnp.concatenate([tr[j][:, 0:hk], q[j] * eg[j]], axis=0).astype(BF16), s0[j].astype(BF16)) for j in heads]
    u = [(tr[j][:, hk:hk + hv] - ps[j][0:c]).astype(BF16) for j in heads]
    for j in heads:
        s_ref[j] = jnp.exp(g_last[j]) * s0[j] + _tdot(k_tail[j], u[j])
    o = [ps[j][c:2 * c] + _dot(qk[j].astype(BF16), u[j]) for j in heads]
    for j in heads:
        cols = slice(j * hv, (j + 1) * hv)
        og_ref[:, cols] = _gated_norm(o[j], z_ref[:, cols], nw_ref[...]).astype(og_ref.dtype)

    @pl.when(ci == pl.num_programs(2) - 1)
    def _():
        so_ref[...] = s_ref[...]


def _delta_prompt(qkv, gb, h_main, nw, l, col_z, nb, t, nh, hk, hv, hg):
    c = DELTA_CHUNK
    assert t % c == 0 and nh % hg == 0 and col_z % (hg * hv) == 0 and hk == V7X_LANES and c <= V7X_LANES // 2
    ncs = t // c
    ngr = nh // hg
    body = functools.partial(_delta_prompt_body, hg=hg, c=c, hk=hk, hv=hv)

    def head_spec(sec):
        return pl.BlockSpec((hg, c, hk), lambda b, g, ci: (sec * ngr + g, b * ncs + ci, 0))

    return pl.pallas_call(
        body,
        grid=(nb, ngr, ncs),
        in_specs=[
            head_spec(0), head_spec(1), head_spec(2),
            pl.BlockSpec((hg, c, V7X_LANES), lambda b, g, ci: (g, b * ncs + ci, 0)),
            pl.BlockSpec((c, hg * hv), lambda b, g, ci: (b * ncs + ci, col_z // (hg * hv) + g)),
            pl.BlockSpec((None, 1, hv), lambda b, g, ci: (l, 0, 0)),
        ],
        out_specs=[
            pl.BlockSpec((c, hg * hv), lambda b, g, ci: (b * ncs + ci, g)),
            pl.BlockSpec((None, hg, hk, hv), lambda b, g, ci: (b, g, 0, 0)),
        ],
        out_shape=[jax.ShapeDtypeStruct((nb * t, nh * hv), BF16), jax.ShapeDtypeStruct((nb, nh, hk, hv), F32)],
        scratch_shapes=[pltpu.VMEM((hg, hk, hv), F32)],
        compiler_params=_params("parallel", "parallel", "arbitrary"),
        name="delta_prompt",
    )(qkv, qkv, qkv, gb, h_main, nw)


def _delta_sample_body(q_ref, k_ref, v_ref, gb_ref, z_ref, nw_ref, s_ref, og_ref, so_ref,
                       kq_scr, p_scr, u_scr, kt_scr, e_scr, *, bs, ts, bb, hk, hv):
    b0 = pl.multiple_of(pl.program_id(1) * bb, bb)
    half = V7X_LANES // 2
    pad = V7X_SUBLANES - ts

    def rows(ref, t):
        return ref[pl.ds(t * bs + b0, bb), :]

    ks = [rows(k_ref, t) for t in range(ts)]
    qs = [rows(q_ref, t) for t in range(ts)]
    for t in range(ts):
        kq_scr[t * bb:(t + 1) * bb, :] = ks[t]
        kq_scr[(ts + t) * bb:(ts + t + 1) * bb, :] = qs[t]

    def state_products(g, carry):
        seqs = [g * SAMPLE_GROUP + i for i in range(SAMPLE_GROUP)]
        kq = [kq_scr[pl.ds(b, 2 * ts, stride=bb), :].astype(BF16) for b in seqs]
        s16 = [s_ref[b].astype(BF16) for b in seqs]
        prod = [_dot(kq[i], s16[i]) for i in range(SAMPLE_GROUP)]
        for i, b in enumerate(seqs):
            p_scr[pl.ds(b, 2 * ts, stride=bb), :] = prod[i]
        return carry

    lax.fori_loop(0, bb // SAMPLE_GROUP, state_products, 0)

    gs = [rows(gb_ref, t)[:, 0:1] for t in range(ts)]
    betas = [rows(gb_ref, t)[:, half:half + 1] for t in range(ts)]
    us = []
    for t in range(ts):
        u = betas[t] * (rows(v_ref, t) - jnp.exp(gs[t]) * p_scr[t * bb:(t + 1) * bb, :])
        for s in range(t):
            kk = jnp.sum(ks[t] * ks[s], axis=-1, keepdims=True)
            u = u - (betas[t] * kk * jnp.exp(gs[t] - gs[s])) * us[s]
        us.append(u)
    for t in range(ts):
        o = jnp.exp(gs[t]) * p_scr[(ts + t) * bb:(ts + t + 1) * bb, :]
        for s in range(t + 1):
            qk = jnp.sum(qs[t] * ks[s], axis=-1, keepdims=True)
            o = o + (qk * jnp.exp(gs[t] - gs[s])) * us[s]
        og_ref[pl.ds(t * bs + b0, bb), :] = _gated_norm(o, rows(z_ref, t), nw_ref[...]).astype(og_ref.dtype)
        u_scr[t * bb:(t + 1) * bb, :] = us[t]
        kt_scr[t * bb:(t + 1) * bb, :] = ks[t] * jnp.exp(gs[ts - 1] - gs[t])
    u_scr[ts * bb:(ts + pad) * bb, :] = jnp.zeros((pad * bb, hv), F32)
    kt_scr[ts * bb:(ts + pad) * bb, :] = jnp.zeros((pad * bb, hk), F32)
    e_scr[...] = jnp.broadcast_to(jnp.exp(gs[ts - 1]), (bb, hv))

    def state_update(g, carry):
        seqs = [g * SAMPLE_GROUP + i for i in range(SAMPLE_GROUP)]
        ub = [u_scr[pl.ds(b, V7X_SUBLANES, stride=bb), :].astype(BF16) for b in seqs]
        kb = [kt_scr[pl.ds(b, V7X_SUBLANES, stride=bb), :].astype(BF16) for b in seqs]
        upd = [_tdot(kb[i], ub[i]) for i in range(SAMPLE_GROUP)]
        for i, b in enumerate(seqs):
            so_ref[b] = e_scr[pl.ds(b, 1), :] * s_ref[b] + upd[i]
        return carry

    lax.fori_loop(0, bb // SAMPLE_GROUP, state_update, 0)


def _delta_sample(qkv, gb, h_main, nw, state, l, col_z, row0, bs, ts, nh, hk, hv):
    tbs = bs * ts
    bb = _divisor(bs, 64, V7X_BF16_ROWS)
    assert row0 % tbs == 0 and col_z % hv == 0 and 2 * ts <= V7X_SUBLANES and bb % SAMPLE_GROUP == 0
    body = functools.partial(_delta_sample_body, bs=bs, ts=ts, bb=bb, hk=hk, hv=hv)

    def head_spec(sec):
        return pl.BlockSpec((None, tbs, hk), lambda h, g: (sec * nh + h, row0 // tbs, 0))

    return pl.pallas_call(
        body,
        grid=(nh, bs // bb),
        in_specs=[
            head_spec(0), head_spec(1), head_spec(2),
            pl.BlockSpec((None, tbs, V7X_LANES), lambda h, g: (h, row0 // tbs, 0)),
            pl.BlockSpec((tbs, hv), lambda h, g: (row0 // tbs, col_z // hv + h)),
            pl.BlockSpec((None, 1, hv), lambda h, g: (l, 0, 0)),
            pl.BlockSpec((None, bb, None, hk, hv), lambda h, g: (l, g, h, 0, 0)),
        ],
        out_specs=[
            pl.BlockSpec((tbs, hv), lambda h, g: (0, h)),
            pl.BlockSpec((bb, None, hk, hv), lambda h, g: (g, h, 0, 0)),
        ],
        out_shape=[jax.ShapeDtypeStruct((tbs, nh * hv), BF16), jax.ShapeDtypeStruct((bs, nh, hk, hv), F32)],
        scratch_shapes=[
            pltpu.VMEM((2 * ts * bb, hk), F32),
            pltpu.VMEM((2 * ts * bb, hv), F32),
            pltpu.VMEM((V7X_SUBLANES * bb, hv), F32),
            pltpu.VMEM((V7X_SUBLANES * bb, hk), F32),
            pltpu.VMEM((bb, hv), F32),
        ],
        compiler_params=_params("parallel", "arbitrary"),
        name="delta_sample",
    )(qkv, qkv, qkv, gb, h_main, nw, state)


def kernel(x_prompt, x_sample, state_pool, state_conv, state_delta, w_in, w_pool_grp, pool_scale, w_pool_up,
           conv_w, a_log, dt_bias, o_norm_w, w_delta_up, w_out, ln1_g, ln1_b, w_gate, w_up, w_down, ln2_g, ln2_b):
    nb, t, d = x_prompt.shape
    bs, ts, _ = x_sample.shape
    depth = w_in.shape[0]
    pb, dp = state_pool.shape[2], state_pool.shape[3]
    cwm1 = state_conv.shape[2]
    assert t >= pb and t >= cwm1
    dqkv = state_conv.shape[3]
    nh, hk, hv = state_delta.shape[2], state_delta.shape[3], state_delta.shape[4]
    dv = nh * hv
    dff = w_gate.shape[-1]
    assert dqkv == 2 * nh * hk + dv and hk == hv
    mp, ms = nb * t, bs * ts
    m = mp + ms
    tb = ms
    assert t % tb == 0 and tb % DELTA_CHUNK == 0
    steps_per_seq = t // tb
    n_prompt_steps = mp // tb
    alpha = float((2 * depth) ** 0.25)

    off_qkv = dp
    off_z = off_qkv + dqkv
    off_beta = off_z + dv
    off_ga = off_beta + 2 * nh
    off_gb = off_ga + d
    assert w_in.shape[-1] == off_gb + d
    col_qkv, col_z, col_pool = 0, dqkv, dqkv + dv
    col_ga = col_pool + dp
    col_gb = col_ga + d
    w_main = jnp.concatenate(
        [w_in[..., off_qkv:off_z], w_in[..., off_z:off_beta], w_in[..., 0:dp], w_in[..., off_ga:]],
        axis=-1).astype(BF16)
    w_small = jnp.pad(w_in[..., off_beta:off_ga], ((0, 0), (0, 0), (0, V7X_LANES - 2 * nh))).astype(BF16)
    w_gate16 = w_gate.astype(BF16)
    w_up16 = w_up.astype(BF16)
    w_down16 = w_down.astype(BF16)
    w_grp16 = w_pool_grp.astype(BF16)
    w_pu16 = w_pool_up.astype(BF16)
    w_du16 = w_delta_up.astype(BF16)
    w_out16 = w_out.astype(BF16)

    lane_pad = ((0, 0), (nh, V7X_LANES - 2 * nh))
    alog_row = jnp.pad(a_log, lane_pad)[:, None, :]
    dtb_row = jnp.pad(dt_bias, lane_pad)[:, None, :]
    scale_row = pool_scale[:, None, :]
    nw_row = o_norm_w[:, None, :]
    ln1_g, ln1_b, ln2_g, ln2_b = (a[:, None, :] for a in (ln1_g, ln1_b, ln2_g, ln2_b))
    st_pool_tm = jnp.swapaxes(state_pool, 1, 2)
    st_conv_tm = jnp.swapaxes(state_conv, 1, 2)

    x = jnp.concatenate([x_prompt.reshape(mp, d), jnp.swapaxes(x_sample, 0, 1).reshape(ms, d)], axis=0)
    x16 = x.astype(BF16)

    pool_p, conv_p, delta_p, pool_s, conv_s, delta_s = [], [], [], [], [], []
    for l in range(depth):
        h_main, h_small = _inproj(x16, w_main, w_small, l)
        yp = _pool(h_main, st_pool_tm, w_grp16, scale_row, l, col_pool, tb, bs, ts, n_prompt_steps, steps_per_seq)
        qkv, gb = _prep(h_main, h_small, st_conv_tm, conv_w, alog_row, dtb_row, l, col_qkv, tb, bs, ts, nh, hk,
                        n_prompt_steps, steps_per_seq)
        og_p, s_p = _delta_prompt(qkv, gb, h_main, nw_row, l, col_z, nb, t, nh, hk, hv, hg=min(nh, 16))
        og_s, s_s = _delta_sample(qkv, gb, h_main, nw_row, state_delta, l, col_z, mp, bs, ts, nh, hk, hv)
        merged = _merge(yp, og_p, og_s, w_pu16, w_du16, h_main, l, col_ga, col_gb, tb)
        r1 = _outproj(merged, w_out16, x, l, alpha)
        x1, x1_16 = _layer_norm(r1, ln1_g, ln1_b, l)
        act = _glu(x1_16, w_gate16, w_up16, l)
        r2 = _down(act, w_down16, x1, l, alpha)
        x, x16 = _layer_norm(r2, ln2_g, ln2_b, l)

        def tail_rows(col, width, keep):
            return jnp.stack([h_main[(b + 1) * t - keep:(b + 1) * t, col:col + width] for b in range(nb)])

        def sample_rows(col, width):
            return jnp.swapaxes(h_main[mp:, col:col + width].reshape(ts, bs, width), 0, 1)

        pool_p.append(tail_rows(col_pool, dp, pb))
        conv_p.append(tail_rows(col_qkv, dqkv, cwm1))
        pool_s.append(jnp.concatenate([state_pool[l], sample_rows(col_pool, dp)], axis=1)[:, -pb:])
        conv_s.append(jnp.concatenate([state_conv[l], sample_rows(col_qkv, dqkv)], axis=1)[:, -cwm1:])
        delta_p.append(s_p)
        delta_s.append(s_s)

    y_prompt = x[:mp].reshape(nb, t, d)
    y_sample = jnp.swapaxes(x[mp:].reshape(ts, bs, d), 0, 1)
    return (y_prompt, y_sample, jnp.stack(pool_p), jnp.stack(conv_p), jnp.stack(delta_p),
            jnp.stack(pool_s), jnp.stack(conv_s), jnp.stack(delta_s))
```

```python
import functools
import math

import jax
import jax.numpy as jnp
from jax import lax
from jax.experimental import pallas as pl
from jax.experimental.pallas import tpu as pltpu

F32 = jnp.float32
BF16 = jnp.bfloat16

POOL_WINDOWS = (2, 4, 8, 16)
DELTA_CHUNK = 64
PAST_LEN = 16384
LN_EPS = 1e-5
RMS_EPS = 1e-6

V7X_LANES = 128
V7X_SUBLANES = 8
V7X_BF16_ROWS = 16
V7X_VMEM_LIMIT_BYTES = 56 * 1024 * 1024
POOL_HALO = 16
CONV_HALO = V7X_SUBLANES
GLU_BLOCK_N = 256
SAMPLE_GROUP = 8


def _params(*semantics):
    return pltpu.CompilerParams(dimension_semantics=semantics, vmem_limit_bytes=V7X_VMEM_LIMIT_BYTES)


def _divisor(n, target, mult):
    best = None
    for d in range(mult, min(n, target) + 1, mult):
        if n % d == 0:
            best = d
    assert best is not None, (n, target, mult)
    return best


def _dot(a, b):
    return jnp.dot(a, b, preferred_element_type=F32)


def _dot_t(a, b):
    return lax.dot_general(a, b, (((1,), (1,)), ((), ())), preferred_element_type=F32)


def _tdot(a, b):
    return lax.dot_general(a, b, (((0,), (0,)), ((), ())), preferred_element_type=F32)


def _softplus(x):
    return jnp.maximum(x, 0.0) + jnp.log1p(jnp.exp(-jnp.abs(x)))


def _repack_body(w_ref, o_ref, os_ref, *, segments, small_off, small_width):
    for src, dst, width in segments:
        o_ref[:, dst:dst + width] = w_ref[:, src:src + width].astype(o_ref.dtype)
    lane = lax.broadcasted_iota(jnp.int32, os_ref.shape, 1)
    os_ref[...] = jnp.where(lane < small_width, w_ref[:, small_off:small_off + V7X_LANES], 0.0).astype(os_ref.dtype)


def _repack_w_in(w_in, segments, small_off, small_width):
    depth, K, n_in = w_in.shape
    n_out = sum(width for _, _, width in segments)
    assert small_off % V7X_LANES == 0 and small_off + V7X_LANES <= n_in
    rows = _divisor(K, 128, V7X_BF16_ROWS)
    body = functools.partial(_repack_body, segments=segments, small_off=small_off, small_width=small_width)
    return pl.pallas_call(
        body,
        grid=(depth, K // rows),
        in_specs=[pl.BlockSpec((None, rows, n_in), lambda l, r: (l, r, 0))],
        out_specs=[
            pl.BlockSpec((None, rows, n_out), lambda l, r: (l, r, 0)),
            pl.BlockSpec((None, rows, V7X_LANES), lambda l, r: (l, r, 0)),
        ],
        out_shape=[jax.ShapeDtypeStruct((depth, K, n_out), BF16), jax.ShapeDtypeStruct((depth, K, V7X_LANES), BF16)],
        compiler_params=_params("parallel", "parallel"),
        name="repack_w_in",
    )(w_in)


def _inproj_body(x_ref, w_ref, ws_ref, o_ref, os_ref):
    x = x_ref[...]
    o_ref[...] = _dot(x, w_ref[...])

    @pl.when(pl.program_id(1) == 0)
    def _():
        os_ref[...] = _dot(x, ws_ref[...])


def _inproj(x, w_main, w_small, l):
    M, K = x.shape
    N = w_main.shape[-1]
    NS = w_small.shape[-1]
    bm = _divisor(M, 1088, V7X_BF16_ROWS)
    bn = _divisor(N, 1024, V7X_LANES)
    return pl.pallas_call(
        _inproj_body,
        grid=(M // bm, N // bn),
        in_specs=[
            pl.BlockSpec((bm, K), lambda i, j: (i, 0)),
            pl.BlockSpec((None, K, bn), lambda i, j: (l, 0, j)),
            pl.BlockSpec((None, K, NS), lambda i, j: (l, 0, 0)),
        ],
        out_specs=[
            pl.BlockSpec((bm, bn), lambda i, j: (i, j)),
            pl.BlockSpec((bm, NS), lambda i, j: (i, 0)),
        ],
        out_shape=[jax.ShapeDtypeStruct((M, N), F32), jax.ShapeDtypeStruct((M, NS), F32)],
        compiler_params=_params("parallel", "arbitrary"),
        name="inproj",
    )(x, w_main, w_small)


def _merge_body(yp_ref, ogp_ref, ogs_ref, wpu_ref, wdu_ref, ga_ref, gb_ref, o_ref, *, n_prompt_blocks):
    og = jnp.where(pl.program_id(1) < n_prompt_blocks, ogp_ref[...], ogs_ref[...])
    ya = _dot(yp_ref[...], wpu_ref[...])
    yb = _dot(og, wdu_ref[...])
    o_ref[...] = (jax.nn.sigmoid(ga_ref[...]) * ya + jax.nn.sigmoid(gb_ref[...]) * yb).astype(o_ref.dtype)


def _merge(yp, og_p, og_s, w_pool_up, w_delta_up, h_main, l, col_ga, col_gb, tb):
    M, DP = yp.shape
    DV = og_p.shape[1]
    D = w_pool_up.shape[-1]
    bn = _divisor(math.gcd(D, col_ga, col_gb), 1024, V7X_LANES)
    npb = og_p.shape[0] // tb
    assert og_s.shape[0] == tb and col_ga % bn == 0 and col_gb % bn == 0
    return pl.pallas_call(
        functools.partial(_merge_body, n_prompt_blocks=npb),
        grid=(D // bn, M // tb),
        in_specs=[
            pl.BlockSpec((tb, DP), lambda j, i: (i, 0)),
            pl.BlockSpec((tb, DV), lambda j, i: (jnp.minimum(i, npb - 1), 0)),
            pl.BlockSpec((tb, DV), lambda j, i: (0, 0)),
            pl.BlockSpec((None, DP, bn), lambda j, i: (l, 0, j)),
            pl.BlockSpec((None, DV, bn), lambda j, i: (l, 0, j)),
            pl.BlockSpec((tb, bn), lambda j, i: (i, col_ga // bn + j)),
            pl.BlockSpec((tb, bn), lambda j, i: (i, col_gb // bn + j)),
        ],
        out_specs=pl.BlockSpec((tb, bn), lambda j, i: (i, j)),
        out_shape=jax.ShapeDtypeStruct((M, D), BF16),
        compiler_params=_params("parallel", "parallel"),
        name="merge",
    )(yp, og_p, og_s, w_pool_up, w_delta_up, h_main, h_main)


def _outproj_body(a_ref, w_ref, x_ref, o_ref, *, alpha):
    o_ref[...] = alpha * x_ref[...] + _dot(a_ref[...], w_ref[...])


def _outproj(a, w, x, l, alpha):
    M, K = a.shape
    N = w.shape[-1]
    bm = _divisor(M, 1088, V7X_BF16_ROWS)
    bn = _divisor(N, 512, V7X_LANES)
    return pl.pallas_call(
        functools.partial(_outproj_body, alpha=alpha),
        grid=(M // bm, N // bn),
        in_specs=[
            pl.BlockSpec((bm, K), lambda i, j: (i, 0)),
            pl.BlockSpec((None, K, bn), lambda i, j: (l, 0, j)),
            pl.BlockSpec((bm, bn), lambda i, j: (i, j)),
        ],
        out_specs=pl.BlockSpec((bm, bn), lambda i, j: (i, j)),
        out_shape=jax.ShapeDtypeStruct((M, N), F32),
        compiler_params=_params("parallel", "parallel"),
        name="outproj",
    )(a, w, x)


def _glu_body(x_ref, wg_ref, wu_ref, o_ref):
    x = x_ref[...]
    g = _dot(x, wg_ref[...].astype(x.dtype))
    u = _dot(x, wu_ref[...].astype(x.dtype))
    o_ref[...] = (g * jax.nn.sigmoid(g) * u).astype(o_ref.dtype)


def _glu(x, w_gate, w_up, l):
    M, K = x.shape
    N = w_gate.shape[-1]
    bm = _divisor(M, 2176, V7X_BF16_ROWS)
    bn = _divisor(N, GLU_BLOCK_N, V7X_LANES)
    return pl.pallas_call(
        _glu_body,
        grid=(M // bm, N // bn),
        in_specs=[
            pl.BlockSpec((bm, K), lambda i, j: (i, 0), pipeline_mode=pl.Buffered(1)),
            pl.BlockSpec((None, K, bn), lambda i, j: (l, 0, j)),
            pl.BlockSpec((None, K, bn), lambda i, j: (l, 0, j)),
        ],
        out_specs=pl.BlockSpec((bm, bn), lambda i, j: (i, j)),
        out_shape=jax.ShapeDtypeStruct((M, N), BF16),
        compiler_params=_params("parallel", "parallel"),
        name="glu",
    )(x, w_gate, w_up)


def _down_body(a_ref, w_ref, x_ref, o_ref, *, alpha):
    @pl.when(pl.program_id(2) == 0)
    def _():
        o_ref[...] = alpha * x_ref[...]

    o_ref[...] += _dot(a_ref[...], w_ref[...])


def _down(a, w, x, l, alpha):
    M, K = a.shape
    N = w.shape[-1]
    bm = _divisor(M, 1088, V7X_BF16_ROWS)
    bn = _divisor(N, 512, V7X_LANES)
    tk = _divisor(K, 5504, V7X_LANES)
    return pl.pallas_call(
        functools.partial(_down_body, alpha=alpha),
        grid=(M // bm, N // bn, K // tk),
        in_specs=[
            pl.BlockSpec((bm, tk), lambda i, j, k: (i, k)),
            pl.BlockSpec((None, tk, bn), lambda i, j, k: (l, k, j)),
            pl.BlockSpec((bm, bn), lambda i, j, k: (i, j)),
        ],
        out_specs=pl.BlockSpec((bm, bn), lambda i, j, k: (i, j)),
        out_shape=jax.ShapeDtypeStruct((M, N), F32),
        compiler_params=_params("parallel", "parallel", "arbitrary"),
        name="down",
    )(a, w, x)


def _ln_body(r_ref, g_ref, b_ref, xf_ref, xb_ref):
    r = r_ref[...]
    mu = jnp.mean(r, axis=-1, keepdims=True)
    xc = r - mu
    var = jnp.mean(xc * xc, axis=-1, keepdims=True)
    y = xc * lax.rsqrt(var + LN_EPS) * g_ref[...] + b_ref[...]
    xf_ref[...] = y
    xb_ref[...] = y.astype(xb_ref.dtype)


def _layer_norm(r, g, b, l):
    M, D = r.shape
    bm = _divisor(M, 272, V7X_BF16_ROWS)
    return pl.pallas_call(
        _ln_body,
        grid=(M // bm,),
        in_specs=[
            pl.BlockSpec((bm, D), lambda i: (i, 0)),
            pl.BlockSpec((None, 1, D), lambda i: (l, 0, 0)),
            pl.BlockSpec((None, 1, D), lambda i: (l, 0, 0)),
        ],
        out_specs=[pl.BlockSpec((bm, D), lambda i: (i, 0)), pl.BlockSpec((bm, D), lambda i: (i, 0))],
        out_shape=[jax.ShapeDtypeStruct((M, D), F32), jax.ShapeDtypeStruct((M, D), BF16)],
        compiler_params=_params("parallel"),
        name="layer_norm",
    )(r, g, b)


def _pool_body(u_ref, st_ref, wg_ref, sc_ref, o_ref, ext_ref, *, tb, bs, ts, pb, gw, n_prompt_steps, steps_per_seq):
    step = pl.program_id(0)

    def finish(mixed):
        for gi, m in enumerate(mixed):
            cols = slice(gi * gw, (gi + 1) * gw)
            y = _dot(m.astype(BF16), wg_ref[gi])
            o_ref[:, cols] = (y * sc_ref[:, cols]).astype(o_ref.dtype)

    @pl.when(step < n_prompt_steps)
    def _():
        blk = step % steps_per_seq

        @pl.when(blk == 0)
        def _():
            ext_ref[0:POOL_HALO, :] = jnp.zeros((POOL_HALO, ext_ref.shape[1]), F32)

        ext_ref[POOL_HALO:POOL_HALO + tb, :] = u_ref[...]
        pos = blk * tb + lax.broadcasted_iota(jnp.int32, (tb, 1), 0)
        mixed = []
        for gi, w in enumerate(POOL_WINDOWS):
            cols = slice(gi * gw, (gi + 1) * gw)
            cur = ext_ref[POOL_HALO:POOL_HALO + tb, cols]
            win = cur
            for i in range(1, w):
                win = win + ext_ref[POOL_HALO - i:POOL_HALO - i + tb, cols]
            cnt = jnp.minimum(pos + 1, w).astype(F32)
            mixed.append(win / cnt - cur)
        finish(mixed)
        ext_ref[0:POOL_HALO, :] = ext_ref[tb:tb + POOL_HALO, :]

    @pl.when(step >= n_prompt_steps)
    def _():
        mixed = []
        for gi, w in enumerate(POOL_WINDOWS):
            cols = slice(gi * gw, (gi + 1) * gw)
            per_t = []
            for t in range(ts):
                acc = None
                for i in range(w):
                    e = pb + t - i
                    term = st_ref[e, :, cols] if e < pb else u_ref[(e - pb) * bs:(e - pb + 1) * bs, cols]
                    acc = term if acc is None else acc + term
                cnt = float(min(PAST_LEN + t + 1, w))
                per_t.append(acc / cnt - u_ref[t * bs:(t + 1) * bs, cols])
            mixed.append(jnp.concatenate(per_t, axis=0))
        finish(mixed)


def _pool(h_main, st_pool_tm, w_grp, scale, l, col_pool, tb, bs, ts, n_prompt_steps, steps_per_seq):
    M = h_main.shape[0]
    _, pb, _, DP = st_pool_tm.shape
    G, gw = w_grp.shape[1], w_grp.shape[2]
    assert G == len(POOL_WINDOWS) and pb == max(POOL_WINDOWS) - 1 and pb <= POOL_HALO and col_pool % DP == 0
    body = functools.partial(_pool_body, tb=tb, bs=bs, ts=ts, pb=pb, gw=gw,
                             n_prompt_steps=n_prompt_steps, steps_per_seq=steps_per_seq)
    return pl.pallas_call(
        body,
        grid=(M // tb,),
        in_specs=[
            pl.BlockSpec((tb, DP), lambda s: (s, col_pool // DP)),
            pl.BlockSpec((None, pb, bs, DP), lambda s: (l, 0, 0, 0)),
            pl.BlockSpec((None, G, gw, gw), lambda s: (l, 0, 0, 0)),
            pl.BlockSpec((None, 1, DP), lambda s: (l, 0, 0)),
        ],
        out_specs=pl.BlockSpec((tb, DP), lambda s: (s, 0)),
        out_shape=jax.ShapeDtypeStruct((M, DP), BF16),
        scratch_shapes=[pltpu.VMEM((POOL_HALO + tb, DP), F32)],
        compiler_params=_params("arbitrary"),
        name="pool_mixer",
    )(h_main, st_pool_tm, w_grp, scale)


def _prep_body(x_ref, hs_ref, stc_ref, cw_ref, al_ref, dtb_ref, qkv_ref, gb_ref, ext_ref, halo_ref, *,
               tb, bs, ts, cw, nh, hk, chunk, n_prompt_steps, steps_per_seq):
    step = pl.program_id(0)
    sec = pl.program_id(1)
    half = V7X_LANES // 2

    def emit(y):
        @pl.when(sec < 2)
        def _():
            qscale = jnp.where(sec == 0, hk ** -0.5, 1.0).astype(F32)
            for h in range(nh):
                yh = y[:, h * hk:(h + 1) * hk]
                ss = jnp.sum(yh * yh, axis=-1, keepdims=True)
                qkv_ref[h] = yh * (lax.rsqrt(ss + RMS_EPS) * qscale)

        @pl.when(sec == 2)
        def _():
            for h in range(nh):
                qkv_ref[h] = y[:, h * hk:(h + 1) * hk]

    def emit_gates(gcum, beta):
        for h in range(nh):
            gcol = jnp.broadcast_to(gcum[:, nh + h:nh + h + 1], (tb, half))
            bcol = jnp.broadcast_to(beta[:, h:h + 1], (tb, half))
            gb_ref[h] = jnp.concatenate([gcol, bcol], axis=1)

    def gate_values():
        hs = hs_ref[...]
        beta = jax.nn.sigmoid(hs)
        g = -jnp.exp(al_ref[...]) * _softplus(hs + dtb_ref[...])
        return g, beta

    @pl.when(step < n_prompt_steps)
    def _():
        blk = step % steps_per_seq

        @pl.when(blk == 0)
        def _():
            ext_ref[0:CONV_HALO, :] = jnp.zeros((CONV_HALO, ext_ref.shape[1]), F32)

        @pl.when(blk != 0)
        def _():
            ext_ref[0:CONV_HALO, :] = halo_ref[sec]

        ext_ref[CONV_HALO:CONV_HALO + tb, :] = x_ref[...]
        base = CONV_HALO - (cw - 1)
        y = ext_ref[base:base + tb, :] * cw_ref[0:1, :]
        for i in range(1, cw):
            y = y + ext_ref[base + i:base + i + tb, :] * cw_ref[i:i + 1, :]
        halo_ref[sec] = ext_ref[tb:tb + CONV_HALO, :]
        emit(y * jax.nn.sigmoid(y))

        @pl.when(sec == 0)
        def _():
            g, beta = gate_values()
            t_in = lax.broadcasted_iota(jnp.int32, g.shape, 0) % chunk
            d = 1
            while d < chunk:
                g = g + jnp.where(t_in >= d, pltpu.roll(g, d, 0), 0.0)
                d *= 2
            emit_gates(g, beta)

    @pl.when(step >= n_prompt_steps)
    def _():
        def ext_slab(j):
            return stc_ref[j] if j < cw - 1 else x_ref[(j - (cw - 1)) * bs:(j - (cw - 2)) * bs, :]

        per_t = []
        for t in range(ts):
            y = ext_slab(t) * cw_ref[0:1, :]
            for i in range(1, cw):
                y = y + ext_slab(t + i) * cw_ref[i:i + 1, :]
            per_t.append(y)
        y = jnp.concatenate(per_t, axis=0)
        emit(y * jax.nn.sigmoid(y))

        @pl.when(sec == 0)
        def _():
            g, beta = gate_values()
            slabs = [g[0:bs]]
            for t in range(1, ts):
                slabs.append(slabs[-1] + g[t * bs:(t + 1) * bs])
            emit_gates(jnp.concatenate(slabs, axis=0), beta)


def _prep(h_main, h_small, st_conv_tm, conv_w, alog_row, dtb_row, l, col_qkv, tb, bs, ts, nh, hk,
          n_prompt_steps, steps_per_seq):
    M = h_main.shape[0]
    cw = conv_w.shape[1]
    sw = nh * hk
    assert cw - 1 <= CONV_HALO and col_qkv % sw == 0 and 2 * nh <= V7X_LANES
    body = functools.partial(_prep_body, tb=tb, bs=bs, ts=ts, cw=cw, nh=nh, hk=hk, chunk=DELTA_CHUNK,
                             n_prompt_steps=n_prompt_steps, steps_per_seq=steps_per_seq)
    return pl.pallas_call(
        body,
        grid=(M // tb, 3),
        in_specs=[
            pl.BlockSpec((tb, sw), lambda s, c: (s, col_qkv // sw + c)),
            pl.BlockSpec((tb, V7X_LANES), lambda s, c: (s, 0)),
            pl.BlockSpec((None, cw - 1, bs, sw), lambda s, c: (l, 0, 0, c)),
            pl.BlockSpec((None, cw, sw), lambda s, c: (l, 0, c)),
            pl.BlockSpec((None, 1, V7X_LANES), lambda s, c: (l, 0, 0)),
            pl.BlockSpec((None, 1, V7X_LANES), lambda s, c: (l, 0, 0)),
        ],
        out_specs=[
            pl.BlockSpec((nh, tb, hk), lambda s, c: (c, s, 0)),
            pl.BlockSpec((nh, tb, V7X_LANES), lambda s, c: (0, s, 0)),
        ],
        out_shape=[jax.ShapeDtypeStruct((3 * nh, M, hk), F32), jax.ShapeDtypeStruct((nh, M, V7X_LANES), F32)],
        scratch_shapes=[pltpu.VMEM((CONV_HALO + tb, sw), F32), pltpu.VMEM((3, CONV_HALO, sw), F32)],
        compiler_params=_params("arbitrary", "arbitrary"),
        name="delta_prep",
    )(h_main, h_small, st_conv_tm, conv_w, alog_row, dtb_row)


def _gated_norm(o, z, nw):
    o = o * lax.rsqrt(jnp.mean(o * o, axis=-1, keepdims=True) + RMS_EPS) * nw
    return o * (z * jax.nn.sigmoid(z))


def _delta_prompt_body(q_ref, k_ref, v_ref, gb_ref, z_ref, nw_ref, og_ref, so_ref, s_ref, *, hg, c, hk, hv):
    ci = pl.program_id(2)
    half = V7X_LANES // 2

    @pl.when(ci == 0)
    def _():
        s_ref[...] = jnp.zeros(s_ref.shape, F32)

    ti = lax.broadcasted_iota(jnp.int32, (c, c), 0)
    si = lax.broadcasted_iota(jnp.int32, (c, c), 1)
    incl = ti >= si
    strict = ti > si
    heads = range(hg)
    q = [q_ref[j] for j in heads]
    k = [k_ref[j] for j in heads]
    gbv = [gb_ref[j] for j in heads]
    g_col = [x[:, 0:1] for x in gbv]
    b_col = [x[:, half:half + 1] for x in gbv]
    decay = [jnp.where(incl, jnp.exp(jnp.where(incl, x[:, 0:c] - jnp.concatenate([x, x], axis=0).T[0:c, 0:c], 0.0)), 0.0)
             for x in gbv]
    kb = [k[j] * b_col[j] for j in heads]
    k16 = [x.astype(BF16) for x in k]
    a = [jnp.where(strict, _dot_t(kb[j].astype(BF16), k16[j]) * decay[j], 0.0) for j in heads]
    qk = [_dot_t(q[j].astype(BF16), k16[j]) * decay[j] for j in heads]
    n = [-x for x in a]
    p = [_dot(x.astype(BF16), x.astype(BF16)) for x in a]
    span = 4
    while span < c:
        both = [_dot(jnp.concatenate([n[j], p[j]], axis=0).astype(BF16), p[j].astype(BF16)) for j in heads]
        n = [n[j] + p[j] + both[j][0:c] for j in heads]
        p = [both[j][c:2 * c] for j in heads]
        span *= 2
    n = [n[j] + p[j] + _dot(n[j].astype(BF16), p[j].astype(BF16)) for j in heads]
    eg = [jnp.exp(x) for x in g_col]
    rhs = [jnp.concatenate([kb[j] * eg[j], v_ref[j] * b_col[j]], axis=1) for j in heads]
    tr = [rhs[j] + _dot(n[j].astype(BF16), rhs[j].astype(BF16)) for j in heads]
    g_last = [x[c - 1:c, :] for x in g_col]
    k_tail = [(k[j] * jnp.exp(g_last[j] - g_col[j])).astype(BF16) for j in heads]
    s0 = [s_ref[j] for j in heads]
    ps = [_dot(jnp.concatenate([tr[j][:, 0:hk], q[j] * eg[j]], axis=0).astype(BF16), s0[j].astype(BF16)) for j in heads]
    u = [(tr[j][:, hk:hk + hv] - ps[j][0:c]).astype(BF16) for j in heads]
    for j in heads:
        s_ref[j] = jnp.exp(g_last[j]) * s0[j] + _tdot(k_tail[j], u[j])
    o = [ps[j][c:2 * c] + _dot(qk[j].astype(BF16), u[j]) for j in heads]
    for j in heads:
        cols = slice(j * hv, (j + 1) * hv)
        og_ref[:, cols] = _gated_norm(o[j], z_ref[:, cols], nw_ref[...]).astype(og_ref.dtype)

    @pl.when(ci == pl.num_programs(2) - 1)
    def _():
        so_ref[...] = s_ref[...]


def _delta_prompt(qkv, gb, h_main, nw, l, col_z, nb, t, nh, hk, hv, hg):
    c = DELTA_CHUNK
    assert t % c == 0 and nh % hg == 0 and col_z % (hg * hv) == 0 and hk == V7X_LANES and c <= V7X_LANES // 2
    ncs = t // c
    ngr = nh // hg
    body = functools.partial(_delta_prompt_body, hg=hg, c=c, hk=hk, hv=hv)

    def head_spec(sec):
        return pl.BlockSpec((hg, c, hk), lambda b, g, ci: (sec * ngr + g, b * ncs + ci, 0))

    return pl.pallas_call(
        body,
        grid=(nb, ngr, ncs),
        in_specs=[
            head_spec(0), head_spec(1), head_spec(2),
            pl.BlockSpec((hg, c, V7X_LANES), lambda b, g, ci: (g, b * ncs + ci, 0)),
            pl.BlockSpec((c, hg * hv), lambda b, g, ci: (b * ncs + ci, col_z // (hg * hv) + g)),
            pl.BlockSpec((None, 1, hv), lambda b, g, ci: (l, 0, 0)),
        ],
        out_specs=[
            pl.BlockSpec((c, hg * hv), lambda b, g, ci: (b * ncs + ci, g)),
            pl.BlockSpec((None, hg, hk, hv), lambda b, g, ci: (b, g, 0, 0)),
        ],
        out_shape=[jax.ShapeDtypeStruct((nb * t, nh * hv), BF16), jax.ShapeDtypeStruct((nb, nh, hk, hv), F32)],
        scratch_shapes=[pltpu.VMEM((hg, hk, hv), F32)],
        compiler_params=_params("parallel", "parallel", "arbitrary"),
        name="delta_prompt",
    )(qkv, qkv, qkv, gb, h_main, nw)


def _delta_sample_body(q_ref, k_ref, v_ref, gb_ref, z_ref, nw_ref, s_ref, og_ref, so_ref,
                       kq_scr, p_scr, u_scr, kt_scr, e_scr, *, bs, ts, bb, hk, hv):
    b0 = pl.multiple_of(pl.program_id(1) * bb, bb)
    half = V7X_LANES // 2
    pad = V7X_SUBLANES - ts

    def rows(ref, t):
        return ref[pl.ds(t * bs + b0, bb), :]

    ks = [rows(k_ref, t) for t in range(ts)]
    qs = [rows(q_ref, t) for t in range(ts)]
    for t in range(ts):
        kq_scr[t * bb:(t + 1) * bb, :] = ks[t]
        kq_scr[(ts + t) * bb:(ts + t + 1) * bb, :] = qs[t]

    def state_products(g, carry):
        seqs = [g * SAMPLE_GROUP + i for i in range(SAMPLE_GROUP)]
        kq = [kq_scr[pl.ds(b, 2 * ts, stride=bb), :].astype(BF16) for b in seqs]
        s16 = [s_ref[b].astype(BF16) for b in seqs]
        prod = [_dot(kq[i], s16[i]) for i in range(SAMPLE_GROUP)]
        for i, b in enumerate(seqs):
            p_scr[pl.ds(b, 2 * ts, stride=bb), :] = prod[i]
        return carry

    lax.fori_loop(0, bb // SAMPLE_GROUP, state_products, 0)

    gs = [rows(gb_ref, t)[:, 0:1] for t in range(ts)]
    betas = [rows(gb_ref, t)[:, half:half + 1] for t in range(ts)]
    us = []
    for t in range(ts):
        u = betas[t] * (rows(v_ref, t) - jnp.exp(gs[t]) * p_scr[t * bb:(t + 1) * bb, :])
        for s in range(t):
            kk = jnp.sum(ks[t] * ks[s], axis=-1, keepdims=True)
            u = u - (betas[t] * kk * jnp.exp(gs[t] - gs[s])) * us[s]
        us.append(u)
    for t in range(ts):
        o = jnp.exp(gs[t]) * p_scr[(ts + t) * bb:(ts + t + 1) * bb, :]
        for s in range(t + 1):
            qk = jnp.sum(qs[t] * ks[s], axis=-1, keepdims=True)
            o = o + (qk * jnp.exp(gs[t] - gs[s])) * us[s]
        og_ref[pl.ds(t * bs + b0, bb), :] = _gated_norm(o, rows(z_ref, t), nw_ref[...]).astype(og_ref.dtype)
        u_scr[t * bb:(t + 1) * bb, :] = us[t]
        kt_scr[t * bb:(t + 1) * bb, :] = ks[t] * jnp.exp(gs[ts - 1] - gs[t])
    u_scr[ts * bb:(ts + pad) * bb, :] = jnp.zeros((pad * bb, hv), F32)
    kt_scr[ts * bb:(ts + pad) * bb, :] = jnp.zeros((pad * bb, hk), F32)
    e_scr[...] = jnp.broadcast_to(jnp.exp(gs[ts - 1]), (bb, hv))

    def state_update(g, carry):
        seqs = [g * SAMPLE_GROUP + i for i in range(SAMPLE_GROUP)]
        ub = [u_scr[pl.ds(b, V7X_SUBLANES, stride=bb), :].astype(BF16) for b in seqs]
        kb = [kt_scr[pl.ds(b, V7X_SUBLANES, stride=bb), :].astype(BF16) for b in seqs]
        upd = [_tdot(kb[i], ub[i]) for i in range(SAMPLE_GROUP)]
        for i, b in enumerate(seqs):
            so_ref[b] = e_scr[pl.ds(b, 1), :] * s_ref[b] + upd[i]
        return carry

    lax.fori_loop(0, bb // SAMPLE_GROUP, state_update, 0)


def _delta_sample(qkv, gb, h_main, nw, state, l, col_z, row0, bs, ts, nh, hk, hv):
    tbs = bs * ts
    bb = _divisor(bs, 64, V7X_BF16_ROWS)
    assert row0 % tbs == 0 and col_z % hv == 0 and 2 * ts <= V7X_SUBLANES and bb % SAMPLE_GROUP == 0
    body = functools.partial(_delta_sample_body, bs=bs, ts=ts, bb=bb, hk=hk, hv=hv)

    def head_spec(sec):
        return pl.BlockSpec((None, tbs, hk), lambda h, g: (sec * nh + h, row0 // tbs, 0))

    return pl.pallas_call(
        body,
        grid=(nh, bs // bb),
        in_specs=[
            head_spec(0), head_spec(1), head_spec(2),
            pl.BlockSpec((None, tbs, V7X_LANES), lambda h, g: (h, row0 // tbs, 0)),
            pl.BlockSpec((tbs, hv), lambda h, g: (row0 // tbs, col_z // hv + h)),
            pl.BlockSpec((None, 1, hv), lambda h, g: (l, 0, 0)),
            pl.BlockSpec((None, bb, None, hk, hv), lambda h, g: (l, g, h, 0, 0)),
        ],
        out_specs=[
            pl.BlockSpec((tbs, hv), lambda h, g: (0, h)),
            pl.BlockSpec((bb, None, hk, hv), lambda h, g: (g, h, 0, 0)),
        ],
        out_shape=[jax.ShapeDtypeStruct((tbs, nh * hv), BF16), jax.ShapeDtypeStruct((bs, nh, hk, hv), F32)],
        scratch_shapes=[
            pltpu.VMEM((2 * ts * bb, hk), F32),
            pltpu.VMEM((2 * ts * bb, hv), F32),
            pltpu.VMEM((V7X_SUBLANES * bb, hv), F32),
            pltpu.VMEM((V7X_SUBLANES * bb, hk), F32),
            pltpu.VMEM((bb, hv), F32),
        ],
        compiler_params=_params("parallel", "arbitrary"),
        name="delta_sample",
    )(qkv, qkv, qkv, gb, h_main, nw, state)


def kernel(x_prompt, x_sample, state_pool, state_conv, state_delta, w_in, w_pool_grp, pool_scale, w_pool_up,
           conv_w, a_log, dt_bias, o_norm_w, w_delta_up, w_out, ln1_g, ln1_b, w_gate, w_up, w_down, ln2_g, ln2_b):
    nb, t, d = x_prompt.shape
    bs, ts, _ = x_sample.shape
    depth = w_in.shape[0]
    pb, dp = state_pool.shape[2], state_pool.shape[3]
    cwm1 = state_conv.shape[2]
    assert t >= pb and t >= cwm1
    dqkv = state_conv.shape[3]
    nh, hk, hv = state_delta.shape[2], state_delta.shape[3], state_delta.shape[4]
    dv = nh * hv
    dff = w_gate.shape[-1]
    assert dqkv == 2 * nh * hk + dv and hk == hv
    mp, ms = nb * t, bs * ts
    m = mp + ms
    tb = ms
    assert t % tb == 0 and tb % DELTA_CHUNK == 0
    steps_per_seq = t // tb
    n_prompt_steps = mp // tb
    alpha = float((2 * depth) ** 0.25)

    off_qkv = dp
    off_z = off_qkv + dqkv
    off_beta = off_z + dv
    off_ga = off_beta + 2 * nh
    off_gb = off_ga + d
    assert w_in.shape[-1] == off_gb + d
    col_qkv, col_z, col_pool = 0, dqkv, dqkv + dv
    col_ga = col_pool + dp
    col_gb = col_ga + d
    segments = ((off_qkv, col_qkv, dqkv), (off_z, col_z, dv), (0, col_pool, dp), (off_ga, col_ga, 2 * d))
    w_main, w_small = _repack_w_in(w_in, segments, off_beta, 2 * nh)
    w_down16 = w_down.astype(BF16)
    w_grp16 = w_pool_grp.astype(BF16)
    w_pu16 = w_pool_up.astype(BF16)
    w_du16 = w_delta_up.astype(BF16)
    w_out16 = w_out.astype(BF16)

    lane_pad = ((0, 0), (nh, V7X_LANES - 2 * nh))
    alog_row = jnp.pad(a_log, lane_pad)[:, None, :]
    dtb_row = jnp.pad(dt_bias, lane_pad)[:, None, :]
    scale_row = pool_scale[:, None, :]
    nw_row = o_norm_w[:, None, :]
    ln1_g, ln1_b, ln2_g, ln2_b = (a[:, None, :] for a in (ln1_g, ln1_b, ln2_g, ln2_b))
    st_pool_tm = jnp.swapaxes(state_pool, 1, 2)
    st_conv_tm = jnp.swapaxes(state_conv, 1, 2)

    x = jnp.concatenate([x_prompt.reshape(mp, d), jnp.swapaxes(x_sample, 0, 1).reshape(ms, d)], axis=0)
    x16 = x.astype(BF16)

    pool_p, conv_p, delta_p, pool_s, conv_s, delta_s = [], [], [], [], [], []
    for l in range(depth):
        h_main, h_small = _inproj(x16, w_main, w_small, l)
        yp = _pool(h_main, st_pool_tm, w_grp16, scale_row, l, col_pool, tb, bs, ts, n_prompt_steps, steps_per_seq)
        qkv, gb = _prep(h_main, h_small, st_conv_tm, conv_w, alog_row, dtb_row, l, col_qkv, tb, bs, ts, nh, hk,
                        n_prompt_steps, steps_per_seq)
        og_p, s_p = _delta_prompt(qkv, gb, h_main, nw_row, l, col_z, nb, t, nh, hk, hv, hg=min(nh, 16))
        og_s, s_s = _delta_sample(qkv, gb, h_main, nw_row, state_delta, l, col_z, mp, bs, ts, nh, hk, hv)
        merged = _merge(yp, og_p, og_s, w_pu16, w_du16, h_main, l, col_ga, col_gb, tb)
        r1 = _outproj(merged, w_out16, x, l, alpha)
        x1, x1_16 = _layer_norm(r1, ln1_g, ln1_b, l)
        act = _glu(x1_16, w_gate, w_up, l)
        r2 = _down(act, w_down16, x1, l, alpha)
        x, x16 = _layer_norm(r2, ln2_g, ln2_b, l)

        def tail_rows(col, width, keep):
            return jnp.stack([h_main[(b + 1) * t - keep:(b + 1) * t, col:col + width] for b in range(nb)])

        def sample_rows(col, width):
            return jnp.swapaxes(h_main[mp:, col:col + width].reshape(ts, bs, width), 0, 1)

        pool_p.append(tail_rows(col_pool, dp, pb))
        conv_p.append(tail_rows(col_qkv, dqkv, cwm1))
        pool_s.append(jnp.concatenate([state_pool[l], sample_rows(col_pool, dp)], axis=1)[:, -pb:])
        conv_s.append(jnp.concatenate([state_conv[l], sample_rows(col_qkv, dqkv)], axis=1)[:, -cwm1:])
        delta_p.append(s_p)
        delta_s.append(s_s)

    y_prompt = x[:mp].reshape(nb, t, d)
    y_sample = jnp.swapaxes(x[mp:].reshape(ts, bs, d), 0, 1)
    return (y_prompt, y_sample, jnp.stack(pool_p), jnp.stack(conv_p), jnp.stack(delta_p),
            jnp.stack(pool_s), jnp.stack(conv_s), jnp.stack(delta_s))
```

```python
import functools
import math

import jax
import jax.numpy as jnp
from jax import lax
from jax.experimental import pallas as pl
from jax.experimental.pallas import tpu as pltpu

F32 = jnp.float32
BF16 = jnp.bfloat16

POOL_WINDOWS = (2, 4, 8, 16)
DELTA_CHUNK = 64
PAST_LEN = 16384
LN_EPS = 1e-5
RMS_EPS = 1e-6

V7X_LANES = 128
V7X_SUBLANES = 8
V7X_BF16_ROWS = 16
V7X_VMEM_LIMIT_BYTES = 56 * 1024 * 1024
POOL_HALO = 16
CONV_HALO = V7X_SUBLANES
GLU_BLOCK_N = 256
INPROJ_BLOCK_N = 512
SAMPLE_GROUP = 8


def _params(*semantics):
    return pltpu.CompilerParams(dimension_semantics=semantics, vmem_limit_bytes=V7X_VMEM_LIMIT_BYTES)


def _divisor(n, target, mult):
    best = None
    for d in range(mult, min(n, target) + 1, mult):
        if n % d == 0:
            best = d
    assert best is not None, (n, target, mult)
    return best


def _dot(a, b):
    return jnp.dot(a, b, preferred_element_type=F32)


def _dot_t(a, b):
    return lax.dot_general(a, b, (((1,), (1,)), ((), ())), preferred_element_type=F32)


def _tdot(a, b):
    return lax.dot_general(a, b, (((0,), (0,)), ((), ())), preferred_element_type=F32)


def _softplus(x):
    return jnp.maximum(x, 0.0) + jnp.log1p(jnp.exp(-jnp.abs(x)))


def _inproj_body(x_ref, w_ref, o_ref):
    x = x_ref[...]
    o_ref[...] = _dot(x, w_ref[0].astype(x.dtype).T)


def _inproj(x, w_in_t, segments, l):
    M, K = x.shape
    n_out = sum(width for _, _, width in segments)
    bm = _divisor(M, 2176, V7X_BF16_ROWS)
    bn = INPROJ_BLOCK_N
    assert all(src % V7X_SUBLANES == 0 and dst % bn == 0 and width % bn == 0 for src, dst, width in segments)

    def src_row(j):
        col = j * bn
        row = 0
        for src, dst, width in segments:
            row = jnp.where((col >= dst) & (col < dst + width), src + col - dst, row)
        return pl.multiple_of(row, V7X_SUBLANES)

    return pl.pallas_call(
        _inproj_body,
        grid=(M // bm, n_out // bn),
        in_specs=[
            pl.BlockSpec((bm, K), lambda i, j: (i, 0), pipeline_mode=pl.Buffered(1)),
            pl.BlockSpec((pl.Element(1), pl.Element(bn), pl.Element(K)), lambda i, j: (l, src_row(j), 0)),
        ],
        out_specs=pl.BlockSpec((bm, bn), lambda i, j: (i, j)),
        out_shape=jax.ShapeDtypeStruct((M, n_out), F32),
        compiler_params=_params("parallel", "arbitrary"),
        name="inproj",
    )(x, w_in_t)


def _merge_body(yp_ref, ogp_ref, ogs_ref, wpu_ref, wdu_ref, ga_ref, gb_ref, o_ref, *, n_prompt_blocks):
    og = jnp.where(pl.program_id(1) < n_prompt_blocks, ogp_ref[...], ogs_ref[...])
    ya = _dot(yp_ref[...], wpu_ref[...])
    yb = _dot(og, wdu_ref[...])
    o_ref[...] = (jax.nn.sigmoid(ga_ref[...]) * ya + jax.nn.sigmoid(gb_ref[...]) * yb).astype(o_ref.dtype)


def _merge(yp, og_p, og_s, w_pool_up, w_delta_up, h_main, l, col_ga, col_gb, tb):
    M, DP = yp.shape
    DV = og_p.shape[1]
    D = w_pool_up.shape[-1]
    bn = _divisor(math.gcd(D, col_ga, col_gb), 1024, V7X_LANES)
    npb = og_p.shape[0] // tb
    assert og_s.shape[0] == tb and col_ga % bn == 0 and col_gb % bn == 0
    return pl.pallas_call(
        functools.partial(_merge_body, n_prompt_blocks=npb),
        grid=(D // bn, M // tb),
        in_specs=[
            pl.BlockSpec((tb, DP), lambda j, i: (i, 0)),
            pl.BlockSpec((tb, DV), lambda j, i: (jnp.minimum(i, npb - 1), 0)),
            pl.BlockSpec((tb, DV), lambda j, i: (0, 0)),
            pl.BlockSpec((None, DP, bn), lambda j, i: (l, 0, j)),
            pl.BlockSpec((None, DV, bn), lambda j, i: (l, 0, j)),
            pl.BlockSpec((tb, bn), lambda j, i: (i, col_ga // bn + j)),
            pl.BlockSpec((tb, bn), lambda j, i: (i, col_gb // bn + j)),
        ],
        out_specs=pl.BlockSpec((tb, bn), lambda j, i: (i, j)),
        out_shape=jax.ShapeDtypeStruct((M, D), BF16),
        compiler_params=_params("parallel", "parallel"),
        name="merge",
    )(yp, og_p, og_s, w_pool_up, w_delta_up, h_main, h_main)


def _outproj_body(a_ref, w_ref, x_ref, o_ref, *, alpha):
    o_ref[...] = alpha * x_ref[...] + _dot(a_ref[...], w_ref[...])


def _outproj(a, w, x, l, alpha):
    M, K = a.shape
    N = w.shape[-1]
    bm = _divisor(M, 1088, V7X_BF16_ROWS)
    bn = _divisor(N, 512, V7X_LANES)
    return pl.pallas_call(
        functools.partial(_outproj_body, alpha=alpha),
        grid=(M // bm, N // bn),
        in_specs=[
            pl.BlockSpec((bm, K), lambda i, j: (i, 0)),
            pl.BlockSpec((None, K, bn), lambda i, j: (l, 0, j)),
            pl.BlockSpec((bm, bn), lambda i, j: (i, j)),
        ],
        out_specs=pl.BlockSpec((bm, bn), lambda i, j: (i, j)),
        out_shape=jax.ShapeDtypeStruct((M, N), F32),
        compiler_params=_params("parallel", "parallel"),
        name="outproj",
    )(a, w, x)


def _glu_body(x_ref, wg_ref, wu_ref, o_ref):
    x = x_ref[...]
    g = _dot(x, wg_ref[...].astype(x.dtype))
    u = _dot(x, wu_ref[...].astype(x.dtype))
    o_ref[...] = (g * jax.nn.sigmoid(g) * u).astype(o_ref.dtype)


def _glu(x, w_gate, w_up, l):
    M, K = x.shape
    N = w_gate.shape[-1]
    bm = _divisor(M, 2176, V7X_BF16_ROWS)
    bn = _divisor(N, GLU_BLOCK_N, V7X_LANES)
    return pl.pallas_call(
        _glu_body,
        grid=(M // bm, N // bn),
        in_specs=[
            pl.BlockSpec((bm, K), lambda i, j: (i, 0), pipeline_mode=pl.Buffered(1)),
            pl.BlockSpec((None, K, bn), lambda i, j: (l, 0, j)),
            pl.BlockSpec((None, K, bn), lambda i, j: (l, 0, j)),
        ],
        out_specs=pl.BlockSpec((bm, bn), lambda i, j: (i, j)),
        out_shape=jax.ShapeDtypeStruct((M, N), BF16),
        compiler_params=_params("parallel", "parallel"),
        name="glu",
    )(x, w_gate, w_up)


def _down_body(a_ref, w_ref, x_ref, o_ref, *, alpha):
    @pl.when(pl.program_id(2) == 0)
    def _():
        o_ref[...] = alpha * x_ref[...]

    o_ref[...] += _dot(a_ref[...], w_ref[...])


def _down(a, w, x, l, alpha):
    M, K = a.shape
    N = w.shape[-1]
    bm = _divisor(M, 1088, V7X_BF16_ROWS)
    bn = _divisor(N, 512, V7X_LANES)
    tk = _divisor(K, 5504, V7X_LANES)
    return pl.pallas_call(
        functools.partial(_down_body, alpha=alpha),
        grid=(M // bm, N // bn, K // tk),
        in_specs=[
            pl.BlockSpec((bm, tk), lambda i, j, k: (i, k)),
            pl.BlockSpec((None, tk, bn), lambda i, j, k: (l, k, j)),
            pl.BlockSpec((bm, bn), lambda i, j, k: (i, j)),
        ],
        out_specs=pl.BlockSpec((bm, bn), lambda i, j, k: (i, j)),
        out_shape=jax.ShapeDtypeStruct((M, N), F32),
        compiler_params=_params("parallel", "parallel", "arbitrary"),
        name="down",
    )(a, w, x)


def _ln_body(r_ref, g_ref, b_ref, xf_ref, xb_ref):
    r = r_ref[...]
    mu = jnp.mean(r, axis=-1, keepdims=True)
    xc = r - mu
    var = jnp.mean(xc * xc, axis=-1, keepdims=True)
    y = xc * lax.rsqrt(var + LN_EPS) * g_ref[...] + b_ref[...]
    xf_ref[...] = y
    xb_ref[...] = y.astype(xb_ref.dtype)


def _layer_norm(r, g, b, l):
    M, D = r.shape
    bm = _divisor(M, 272, V7X_BF16_ROWS)
    return pl.pallas_call(
        _ln_body,
        grid=(M // bm,),
        in_specs=[
            pl.BlockSpec((bm, D), lambda i: (i, 0)),
            pl.BlockSpec((None, 1, D), lambda i: (l, 0, 0)),
            pl.BlockSpec((None, 1, D), lambda i: (l, 0, 0)),
        ],
        out_specs=[pl.BlockSpec((bm, D), lambda i: (i, 0)), pl.BlockSpec((bm, D), lambda i: (i, 0))],
        out_shape=[jax.ShapeDtypeStruct((M, D), F32), jax.ShapeDtypeStruct((M, D), BF16)],
        compiler_params=_params("parallel"),
        name="layer_norm",
    )(r, g, b)


def _pool_body(u_ref, st_ref, wg_ref, sc_ref, o_ref, ext_ref, *, tb, bs, ts, pb, gw, n_prompt_steps, steps_per_seq):
    step = pl.program_id(0)

    def finish(mixed):
        for gi, m in enumerate(mixed):
            cols = slice(gi * gw, (gi + 1) * gw)
            y = _dot(m.astype(BF16), wg_ref[gi])
            o_ref[:, cols] = (y * sc_ref[:, cols]).astype(o_ref.dtype)

    @pl.when(step < n_prompt_steps)
    def _():
        blk = step % steps_per_seq

        @pl.when(blk == 0)
        def _():
            ext_ref[0:POOL_HALO, :] = jnp.zeros((POOL_HALO, ext_ref.shape[1]), F32)

        ext_ref[POOL_HALO:POOL_HALO + tb, :] = u_ref[...]
        pos = blk * tb + lax.broadcasted_iota(jnp.int32, (tb, 1), 0)
        mixed = []
        for gi, w in enumerate(POOL_WINDOWS):
            cols = slice(gi * gw, (gi + 1) * gw)
            cur = ext_ref[POOL_HALO:POOL_HALO + tb, cols]
            win = cur
            for i in range(1, w):
                win = win + ext_ref[POOL_HALO - i:POOL_HALO - i + tb, cols]
            cnt = jnp.minimum(pos + 1, w).astype(F32)
            mixed.append(win / cnt - cur)
        finish(mixed)
        ext_ref[0:POOL_HALO, :] = ext_ref[tb:tb + POOL_HALO, :]

    @pl.when(step >= n_prompt_steps)
    def _():
        mixed = []
        for gi, w in enumerate(POOL_WINDOWS):
            cols = slice(gi * gw, (gi + 1) * gw)
            per_t = []
            for t in range(ts):
                acc = None
                for i in range(w):
                    e = pb + t - i
                    term = st_ref[e, :, cols] if e < pb else u_ref[(e - pb) * bs:(e - pb + 1) * bs, cols]
                    acc = term if acc is None else acc + term
                cnt = float(min(PAST_LEN + t + 1, w))
                per_t.append(acc / cnt - u_ref[t * bs:(t + 1) * bs, cols])
            mixed.append(jnp.concatenate(per_t, axis=0))
        finish(mixed)


def _pool(h_main, st_pool_tm, w_grp, scale, l, col_pool, tb, bs, ts, n_prompt_steps, steps_per_seq):
    M = h_main.shape[0]
    _, pb, _, DP = st_pool_tm.shape
    G, gw = w_grp.shape[1], w_grp.shape[2]
    assert G == len(POOL_WINDOWS) and pb == max(POOL_WINDOWS) - 1 and pb <= POOL_HALO and col_pool % DP == 0
    body = functools.partial(_pool_body, tb=tb, bs=bs, ts=ts, pb=pb, gw=gw,
                             n_prompt_steps=n_prompt_steps, steps_per_seq=steps_per_seq)
    return pl.pallas_call(
        body,
        grid=(M // tb,),
        in_specs=[
            pl.BlockSpec((tb, DP), lambda s: (s, col_pool // DP)),
            pl.BlockSpec((None, pb, bs, DP), lambda s: (l, 0, 0, 0)),
            pl.BlockSpec((None, G, gw, gw), lambda s: (l, 0, 0, 0)),
            pl.BlockSpec((None, 1, DP), lambda s: (l, 0, 0)),
        ],
        out_specs=pl.BlockSpec((tb, DP), lambda s: (s, 0)),
        out_shape=jax.ShapeDtypeStruct((M, DP), BF16),
        scratch_shapes=[pltpu.VMEM((POOL_HALO + tb, DP), F32)],
        compiler_params=_params("arbitrary"),
        name="pool_mixer",
    )(h_main, st_pool_tm, w_grp, scale)


def _prep_body(x_ref, hs_ref, stc_ref, cw_ref, al_ref, dtb_ref, qkv_ref, gb_ref, ext_ref, halo_ref, *,
               tb, bs, ts, cw, nh, hk, chunk, n_prompt_steps, steps_per_seq):
    step = pl.program_id(0)
    sec = pl.program_id(1)
    half = V7X_LANES // 2

    def emit(conv_head):
        @pl.when(sec < 2)
        def _():
            qscale = jnp.where(sec == 0, hk ** -0.5, 1.0).astype(F32)
            for h in range(nh):
                y = conv_head(h)
                y = y * jax.nn.sigmoid(y)
                ss = jnp.sum(y * y, axis=-1, keepdims=True)
                qkv_ref[h] = y * (lax.rsqrt(ss + RMS_EPS) * qscale)

        @pl.when(sec == 2)
        def _():
            for h in range(nh):
                y = conv_head(h)
                qkv_ref[h] = y * jax.nn.sigmoid(y)

    def emit_gates(gcum, beta):
        for h in range(nh):
            gcol = jnp.broadcast_to(gcum[:, nh + h:nh + h + 1], (tb, half))
            bcol = jnp.broadcast_to(beta[:, h:h + 1], (tb, half))
            gb_ref[h] = jnp.concatenate([gcol, bcol], axis=1)

    def gate_values():
        hs = hs_ref[...]
        beta = jax.nn.sigmoid(hs)
        g = -jnp.exp(al_ref[...]) * _softplus(hs + dtb_ref[...])
        return g, beta

    @pl.when(step < n_prompt_steps)
    def _():
        blk = step % steps_per_seq

        @pl.when(blk == 0)
        def _():
            ext_ref[0:CONV_HALO, :] = jnp.zeros((CONV_HALO, ext_ref.shape[1]), F32)

        @pl.when(blk != 0)
        def _():
            ext_ref[0:CONV_HALO, :] = halo_ref[sec]

        ext_ref[CONV_HALO:CONV_HALO + tb, :] = x_ref[...]
        halo_ref[sec] = ext_ref[tb:tb + CONV_HALO, :]
        base = CONV_HALO - (cw - 1)

        def conv_head(h):
            cols = slice(h * hk, (h + 1) * hk)
            y = ext_ref[base:base + tb, cols] * cw_ref[0:1, cols]
            for i in range(1, cw):
                y = y + ext_ref[base + i:base + i + tb, cols] * cw_ref[i:i + 1, cols]
            return y

        emit(conv_head)

        @pl.when(sec == 0)
        def _():
            g, beta = gate_values()
            t_in = lax.broadcasted_iota(jnp.int32, g.shape, 0) % chunk
            d = 1
            while d < chunk:
                g = g + jnp.where(t_in >= d, pltpu.roll(g, d, 0), 0.0)
                d *= 2
            emit_gates(g, beta)

    @pl.when(step >= n_prompt_steps)
    def _():
        def ext_slab(j, cols):
            return stc_ref[j, :, cols] if j < cw - 1 else x_ref[(j - (cw - 1)) * bs:(j - (cw - 2)) * bs, cols]

        def conv_head(h):
            cols = slice(h * hk, (h + 1) * hk)
            per_t = []
            for t in range(ts):
                y = ext_slab(t, cols) * cw_ref[0:1, cols]
                for i in range(1, cw):
                    y = y + ext_slab(t + i, cols) * cw_ref[i:i + 1, cols]
                per_t.append(y)
            return jnp.concatenate(per_t, axis=0)

        emit(conv_head)

        @pl.when(sec == 0)
        def _():
            g, beta = gate_values()
            slabs = [g[0:bs]]
            for t in range(1, ts):
                slabs.append(slabs[-1] + g[t * bs:(t + 1) * bs])
            emit_gates(jnp.concatenate(slabs, axis=0), beta)


def _prep(h_main, st_conv_tm, conv_w, alog_row, dtb_row, l, col_qkv, col_small, tb, bs, ts, nh, hk,
          n_prompt_steps, steps_per_seq):
    M = h_main.shape[0]
    cw = conv_w.shape[1]
    sw = nh * hk
    assert cw - 1 <= CONV_HALO and col_qkv % sw == 0 and 2 * nh <= V7X_LANES and col_small % V7X_LANES == 0
    body = functools.partial(_prep_body, tb=tb, bs=bs, ts=ts, cw=cw, nh=nh, hk=hk, chunk=DELTA_CHUNK,
                             n_prompt_steps=n_prompt_steps, steps_per_seq=steps_per_seq)
    return pl.pallas_call(
        body,
        grid=(M // tb, 3),
        in_specs=[
            pl.BlockSpec((tb, sw), lambda s, c: (s, col_qkv // sw + c)),
            pl.BlockSpec((tb, V7X_LANES), lambda s, c: (s, col_small // V7X_LANES)),
            pl.BlockSpec((None, cw - 1, bs, sw), lambda s, c: (l, 0, 0, c)),
            pl.BlockSpec((None, cw, sw), lambda s, c: (l, 0, c)),
            pl.BlockSpec((None, 1, V7X_LANES), lambda s, c: (l, 0, 0)),
            pl.BlockSpec((None, 1, V7X_LANES), lambda s, c: (l, 0, 0)),
        ],
        out_specs=[
            pl.BlockSpec((nh, tb, hk), lambda s, c: (c, s, 0)),
            pl.BlockSpec((nh, tb, V7X_LANES), lambda s, c: (0, s, 0)),
        ],
        out_shape=[jax.ShapeDtypeStruct((3 * nh, M, hk), F32), jax.ShapeDtypeStruct((nh, M, V7X_LANES), F32)],
        scratch_shapes=[pltpu.VMEM((CONV_HALO + tb, sw), F32), pltpu.VMEM((3, CONV_HALO, sw), F32)],
        compiler_params=_params("arbitrary", "arbitrary"),
        name="delta_prep",
    )(h_main, h_main, st_conv_tm, conv_w, alog_row, dtb_row)


def _gated_norm(o, z, nw):
    o = o * lax.rsqrt(jnp.mean(o * o, axis=-1, keepdims=True) + RMS_EPS) * nw
    return o * (z * jax.nn.sigmoid(z))


def _delta_prompt_body(q_ref, k_ref, v_ref, gb_ref, z_ref, nw_ref, og_ref, so_ref, s_ref, *, hg, c, hk, hv):
    ci = pl.program_id(2)
    half = V7X_LANES // 2

    @pl.when(ci == 0)
    def _():
        s_ref[...] = jnp.zeros(s_ref.shape, F32)

    ti = lax.broadcasted_iota(jnp.int32, (c, c), 0)
    si = lax.broadcasted_iota(jnp.int32, (c, c), 1)
    incl = ti >= si
    strict = ti > si
    heads = range(hg)
    q = [q_ref[j] for j in heads]
    k = [k_ref[j] for j in heads]
    gbv = [gb_ref[j] for j in heads]
    g_col = [x[:, 0:1] for x in gbv]
    b_col = [x[:, half:half + 1] for x in gbv]
    decay = [jnp.where(incl, jnp.exp(jnp.where(incl, x[:, 0:c] - jnp.concatenate([x, x], axis=0).T[0:c, 0:c], 0.0)), 0.0)
             for x in gbv]
    kb = [k[j] * b_col[j] for j in heads]
    k16 = [x.astype(BF16) for x in k]
    a = [jnp.where(strict, _dot_t(kb[j].astype(BF16), k16[j]) * decay[j], 0.0) for j in heads]
    qk = [_dot_t(q[j].astype(BF16), k16[j]) * decay[j] for j in heads]
    n = [-x for x in a]
    p = [_dot(x.astype(BF16), x.astype(BF16)) for x in a]
    span = 4
    while span < c:
        both = [_dot(jnp.concatenate([n[j], p[j]], axis=0).astype(BF16), p[j].astype(BF16)) for j in heads]
        n = [n[j] + p[j] + both[j][0:c] for j in heads]
        p = [both[j][c:2 * c] for j in heads]
        span *= 2
    n = [n[j] + p[j] + _dot(n[j].astype(BF16), p[j].astype(BF16)) for j in heads]
    eg = [jnp.exp(x) for x in g_col]
    rhs = [jnp.concatenate([kb[j] * eg[j], v_ref[j] * b_col[j]], axis=1) for j in heads]
    tr = [rhs[j] + _dot(n[j].astype(BF16), rhs[j].astype(BF16)) for j in heads]
    g_last = [x[c - 1:c, :] for x in g_col]
    k_tail = [(k[j] * jnp.exp(g_last[j] - g_col[j])).astype(BF16) for j in heads]
    s0 = [s_ref[j] for j in heads]
    ps = [_dot(jnp.concatenate([tr[j][:, 0:hk], q[j] * eg[j]], axis=0).astype(BF16), s0[j].astype(BF16)) for j in heads]
    u = [(tr[j][:, hk:hk + hv] - ps[j][0:c]).astype(BF16) for j in heads]
    for j in heads:
        s_ref[j] = jnp.exp(g_last[j]) * s0[j] + _tdot(k_tail[j], u[j])
    o = [ps[j][c:2 * c] + _dot(qk[j].astype(BF16), u[j]) for j in heads]
    for j in heads:
        cols = slice(j * hv, (j + 1) * hv)
        og_ref[:, cols] = _gated_norm(o[j], z_ref[:, cols], nw_ref[...]).astype(og_ref.dtype)

    @pl.when(ci == pl.num_programs(2) - 1)
    def _():
        so_ref[...] = s_ref[...]


def _delta_prompt(qkv, gb, h_main, nw, l, col_z, nb, t, nh, hk, hv, hg):
    c = DELTA_CHUNK
    assert t % c == 0 and nh % hg == 0 and col_z % (hg * hv) == 0 and hk == V7X_LANES and c <= V7X_LANES // 2
    ncs = t // c
    ngr = nh // hg
    body = functools.partial(_delta_prompt_body, hg=hg, c=c, hk=hk, hv=hv)

    def head_spec(sec):
        return pl.BlockSpec((hg, c, hk), lambda b, g, ci: (sec * ngr + g, b * ncs + ci, 0))

    return pl.pallas_call(
        body,
        grid=(nb, ngr, ncs),
        in_specs=[
            head_spec(0), head_spec(1), head_spec(2),
            pl.BlockSpec((hg, c, V7X_LANES), lambda b, g, ci: (g, b * ncs + ci, 0)),
            pl.BlockSpec((c, hg * hv), lambda b, g, ci: (b * ncs + ci, col_z // (hg * hv) + g)),
            pl.BlockSpec((None, 1, hv), lambda b, g, ci: (l, 0, 0)),
        ],
        out_specs=[
            pl.BlockSpec((c, hg * hv), lambda b, g, ci: (b * ncs + ci, g)),
            pl.BlockSpec((None, hg, hk, hv), lambda b, g, ci: (b, g, 0, 0)),
        ],
        out_shape=[jax.ShapeDtypeStruct((nb * t, nh * hv), BF16), jax.ShapeDtypeStruct((nb, nh, hk, hv), F32)],
        scratch_shapes=[pltpu.VMEM((hg, hk, hv), F32)],
        compiler_params=_params("parallel", "parallel", "arbitrary"),
        name="delta_prompt",
    )(qkv, qkv, qkv, gb, h_main, nw)


def _delta_sample_body(q_ref, k_ref, v_ref, gb_ref, z_ref, nw_ref, s_ref, og_ref, so_ref,
                       kq_scr, p_scr, u_scr, kt_scr, e_scr, *, bs, ts, bb, hk, hv):
    b0 = pl.multiple_of(pl.program_id(1) * bb, bb)
    half = V7X_LANES // 2
    pad = V7X_SUBLANES - ts

    def rows(ref, t):
        return ref[pl.ds(t * bs + b0, bb), :]

    ks = [rows(k_ref, t) for t in range(ts)]
    qs = [rows(q_ref, t) for t in range(ts)]
    for t in range(ts):
        kq_scr[t * bb:(t + 1) * bb, :] = ks[t]
        kq_scr[(ts + t) * bb:(ts + t + 1) * bb, :] = qs[t]

    def state_products(g, carry):
        seqs = [g * SAMPLE_GROUP + i for i in range(SAMPLE_GROUP)]
        kq = [kq_scr[pl.ds(b, 2 * ts, stride=bb), :].astype(BF16) for b in seqs]
        s16 = [s_ref[b].astype(BF16) for b in seqs]
        prod = [_dot(kq[i], s16[i]) for i in range(SAMPLE_GROUP)]
        for i, b in enumerate(seqs):
            p_scr[pl.ds(b, 2 * ts, stride=bb), :] = prod[i]
        return carry

    lax.fori_loop(0, bb // SAMPLE_GROUP, state_products, 0)

    gs = [rows(gb_ref, t)[:, 0:1] for t in range(ts)]
    betas = [rows(gb_ref, t)[:, half:half + 1] for t in range(ts)]
    us = []
    for t in range(ts):
        u = betas[t] * (rows(v_ref, t) - jnp.exp(gs[t]) * p_scr[t * bb:(t + 1) * bb, :])
        for s in range(t):
            kk = jnp.sum(ks[t] * ks[s], axis=-1, keepdims=True)
            u = u - (betas[t] * kk * jnp.exp(gs[t] - gs[s])) * us[s]
        us.append(u)
    for t in range(ts):
        o = jnp.exp(gs[t]) * p_scr[(ts + t) * bb:(ts + t + 1) * bb, :]
        for s in range(t + 1):
            qk = jnp.sum(qs[t] * ks[s], axis=-1, keepdims=True)
            o = o + (qk * jnp.exp(gs[t] - gs[s])) * us[s]
        og_ref[pl.ds(t * bs + b0, bb), :] = _gated_norm(o, rows(z_ref, t), nw_ref[...]).astype(og_ref.dtype)
        u_scr[t * bb:(t + 1) * bb, :] = us[t]
        kt_scr[t * bb:(t + 1) * bb, :] = ks[t] * jnp.exp(gs[ts - 1] - gs[t])
    u_scr[ts * bb:(ts + pad) * bb, :] = jnp.zeros((pad * bb, hv), F32)
    kt_scr[ts * bb:(ts + pad) * bb, :] = jnp.zeros((pad * bb, hk), F32)
    e_scr[...] = jnp.broadcast_to(jnp.exp(gs[ts - 1]), (bb, hv))

    def state_update(g, carry):
        seqs = [g * SAMPLE_GROUP + i for i in range(SAMPLE_GROUP)]
        ub = [u_scr[pl.ds(b, V7X_SUBLANES, stride=bb), :].astype(BF16) for b in seqs]
        kb = [kt_scr[pl.ds(b, V7X_SUBLANES, stride=bb), :].astype(BF16) for b in seqs]
        upd = [_tdot(kb[i], ub[i]) for i in range(SAMPLE_GROUP)]
        for i, b in enumerate(seqs):
            so_ref[b] = e_scr[pl.ds(b, 1), :] * s_ref[b] + upd[i]
        return carry

    lax.fori_loop(0, bb // SAMPLE_GROUP, state_update, 0)


def _delta_sample(qkv, gb, h_main, nw, state, l, col_z, row0, bs, ts, nh, hk, hv):
    tbs = bs * ts
    bb = _divisor(bs, 64, V7X_BF16_ROWS)
    assert row0 % tbs == 0 and col_z % hv == 0 and 2 * ts <= V7X_SUBLANES and bb % SAMPLE_GROUP == 0
    body = functools.partial(_delta_sample_body, bs=bs, ts=ts, bb=bb, hk=hk, hv=hv)

    def head_spec(sec):
        return pl.BlockSpec((None, tbs, hk), lambda h, g: (sec * nh + h, row0 // tbs, 0))

    return pl.pallas_call(
        body,
        grid=(nh, bs // bb),
        in_specs=[
            head_spec(0), head_spec(1), head_spec(2),
            pl.BlockSpec((None, tbs, V7X_LANES), lambda h, g: (h, row0 // tbs, 0)),
            pl.BlockSpec((tbs, hv), lambda h, g: (row0 // tbs, col_z // hv + h)),
            pl.BlockSpec((None, 1, hv), lambda h, g: (l, 0, 0)),
            pl.BlockSpec((None, bb, None, hk, hv), lambda h, g: (l, g, h, 0, 0)),
        ],
        out_specs=[
            pl.BlockSpec((tbs, hv), lambda h, g: (0, h)),
            pl.BlockSpec((bb, None, hk, hv), lambda h, g: (g, h, 0, 0)),
        ],
        out_shape=[jax.ShapeDtypeStruct((tbs, nh * hv), BF16), jax.ShapeDtypeStruct((bs, nh, hk, hv), F32)],
        scratch_shapes=[
            pltpu.VMEM((2 * ts * bb, hk), F32),
            pltpu.VMEM((2 * ts * bb, hv), F32),
            pltpu.VMEM((V7X_SUBLANES * bb, hv), F32),
            pltpu.VMEM((V7X_SUBLANES * bb, hk), F32),
            pltpu.VMEM((bb, hv), F32),
        ],
        compiler_params=_params("parallel", "arbitrary"),
        name="delta_sample",
    )(qkv, qkv, qkv, gb, h_main, nw, state)


def kernel(x_prompt, x_sample, state_pool, state_conv, state_delta, w_in, w_pool_grp, pool_scale, w_pool_up,
           conv_w, a_log, dt_bias, o_norm_w, w_delta_up, w_out, ln1_g, ln1_b, w_gate, w_up, w_down, ln2_g, ln2_b):
    nb, t, d = x_prompt.shape
    bs, ts, _ = x_sample.shape
    depth = w_in.shape[0]
    pb, dp = state_pool.shape[2], state_pool.shape[3]
    cwm1 = state_conv.shape[2]
    assert t >= pb and t >= cwm1
    dqkv = state_conv.shape[3]
    nh, hk, hv = state_delta.shape[2], state_delta.shape[3], state_delta.shape[4]
    dv = nh * hv
    dff = w_gate.shape[-1]
    assert dqkv == 2 * nh * hk + dv and hk == hv
    mp, ms = nb * t, bs * ts
    m = mp + ms
    tb = ms
    assert t % tb == 0 and tb % DELTA_CHUNK == 0
    steps_per_seq = t // tb
    n_prompt_steps = mp // tb
    alpha = float((2 * depth) ** 0.25)

    off_qkv = dp
    off_z = off_qkv + dqkv
    off_beta = off_z + dv
    off_ga = off_beta + 2 * nh
    off_gb = off_ga + d
    assert w_in.shape[-1] == off_gb + d
    col_qkv, col_z, col_pool = 0, dqkv, dqkv + dv
    col_ga = col_pool + dp
    col_gb = col_ga + d
    col_small = col_gb + d
    segments = ((off_qkv, col_qkv, dqkv), (off_z, col_z, dv), (0, col_pool, dp), (off_ga, col_ga, 2 * d),
                (off_beta, col_small, INPROJ_BLOCK_N))
    assert off_beta + INPROJ_BLOCK_N <= w_in.shape[-1]
    w_in_t = jnp.swapaxes(w_in, 1, 2)
    w_down16 = w_down.astype(BF16)
    w_grp16 = w_pool_grp.astype(BF16)
    w_pu16 = w_pool_up.astype(BF16)
    w_du16 = w_delta_up.astype(BF16)
    w_out16 = w_out.astype(BF16)

    lane_pad = ((0, 0), (nh, V7X_LANES - 2 * nh))
    alog_row = jnp.pad(a_log, lane_pad)[:, None, :]
    dtb_row = jnp.pad(dt_bias, lane_pad)[:, None, :]
    scale_row = pool_scale[:, None, :]
    nw_row = o_norm_w[:, None, :]
    ln1_g, ln1_b, ln2_g, ln2_b = (a[:, None, :] for a in (ln1_g, ln1_b, ln2_g, ln2_b))
    st_pool_tm = jnp.swapaxes(state_pool, 1, 2)
    st_conv_tm = jnp.swapaxes(state_conv, 1, 2)

    x = jnp.concatenate([x_prompt.reshape(mp, d), jnp.swapaxes(x_sample, 0, 1).reshape(ms, d)], axis=0)
    x16 = x.astype(BF16)

    pool_p, conv_p, delta_p, pool_s, conv_s, delta_s = [], [], [], [], [], []
    for l in range(depth):
        h_main = _inproj(x16, w_in_t, segments, l)
        yp = _pool(h_main, st_pool_tm, w_grp16, scale_row, l, col_pool, tb, bs, ts, n_prompt_steps, steps_per_seq)
        qkv, gb = _prep(h_main, st_conv_tm, conv_w, alog_row, dtb_row, l, col_qkv, col_small, tb, bs, ts, nh, hk,
                        n_prompt_steps, steps_per_seq)
        og_p, s_p = _delta_prompt(qkv, gb, h_main, nw_row, l, col_z, nb, t, nh, hk, hv, hg=min(nh, 16))
        og_s, s_s = _delta_sample(qkv, gb, h_main, nw_row, state_delta, l, col_z, mp, bs, ts, nh, hk, hv)
        merged = _merge(yp, og_p, og_s, w_pu16, w_du16, h_main, l, col_ga, col_gb, tb)
        r1 = _outproj(merged, w_out16, x, l, alpha)
        x1, x1_16 = _layer_norm(r1, ln1_g, ln1_b, l)
        act = _glu(x1_16, w_gate, w_up, l)
        r2 = _down(act, w_down16, x1, l, alpha)
        x, x16 = _layer_norm(r2, ln2_g, ln2_b, l)

        def tail_rows(col, width, keep):
            return jnp.stack([h_main[(b + 1) * t - keep:(b + 1) * t, col:col + width] for b in range(nb)])

        def sample_rows(col, width):
            return jnp.swapaxes(h_main[mp:, col:col + width].reshape(ts, bs, width), 0, 1)

        pool_p.append(tail_rows(col_pool, dp, pb))
        conv_p.append(tail_rows(col_qkv, dqkv, cwm1))
        pool_s.append(jnp.concatenate([state_pool[l], sample_rows(col_pool, dp)], axis=1)[:, -pb:])
        conv_s.append(jnp.concatenate([state_conv[l], sample_rows(col_qkv, dqkv)], axis=1)[:, -cwm1:])
        delta_p.append(s_p)
        delta_s.append(s_s)

    y_prompt = x[:mp].reshape(nb, t, d)
    y_sample = jnp.swapaxes(x[mp:].reshape(ts, bs, d), 0, 1)
    return (y_prompt, y_sample, jnp.stack(pool_p), jnp.stack(conv_p), jnp.stack(delta_p),
            jnp.stack(pool_s), jnp.stack(conv_s), jnp.stack(delta_s))
```

```python
import functools
import math

import jax
import jax.numpy as jnp
from jax import lax
from jax.experimental import pallas as pl
from jax.experimental.pallas import tpu as pltpu

F32 = jnp.float32
BF16 = jnp.bfloat16

POOL_WINDOWS = (2, 4, 8, 16)
DELTA_CHUNK = 64
PAST_LEN = 16384
LN_EPS = 1e-5
RMS_EPS = 1e-6

V7X_LANES = 128
V7X_SUBLANES = 8
V7X_BF16_ROWS = 16
V7X_VMEM_LIMIT_BYTES = 56 * 1024 * 1024
POOL_HALO = 16
CONV_HALO = V7X_SUBLANES
GLU_BLOCK_N = 256
INPROJ_BLOCK_N = 512
SAMPLE_GROUP = 8


def _params(*semantics):
    return pltpu.CompilerParams(dimension_semantics=semantics, vmem_limit_bytes=V7X_VMEM_LIMIT_BYTES)


def _divisor(n, target, mult):
    best = None
    for d in range(mult, min(n, target) + 1, mult):
        if n % d == 0:
            best = d
    assert best is not None, (n, target, mult)
    return best


def _dot(a, b):
    return jnp.dot(a, b, preferred_element_type=F32)


def _dot_t(a, b):
    return lax.dot_general(a, b, (((1,), (1,)), ((), ())), preferred_element_type=F32)


def _tdot(a, b):
    return lax.dot_general(a, b, (((0,), (0,)), ((), ())), preferred_element_type=F32)


def _softplus(x):
    return jnp.maximum(x, 0.0) + jnp.log1p(jnp.exp(-jnp.abs(x)))


def _inproj_body(x_ref, w_ref, o_ref):
    x = x_ref[...]
    o_ref[...] = _dot(x, w_ref[0].astype(x.dtype).T)


def _inproj(x, w_in_t, segments, l):
    M, K = x.shape
    n_out = sum(width for _, _, width in segments)
    bm = _divisor(M, 2176, V7X_BF16_ROWS)
    bn = INPROJ_BLOCK_N
    assert all(src % V7X_SUBLANES == 0 and dst % bn == 0 and width % bn == 0 for src, dst, width in segments)

    def src_row(j):
        col = j * bn
        row = 0
        for src, dst, width in segments:
            row = jnp.where((col >= dst) & (col < dst + width), src + col - dst, row)
        return pl.multiple_of(row, V7X_SUBLANES)

    return pl.pallas_call(
        _inproj_body,
        grid=(M // bm, n_out // bn),
        in_specs=[
            pl.BlockSpec((bm, K), lambda i, j: (i, 0), pipeline_mode=pl.Buffered(1)),
            pl.BlockSpec((pl.Element(1), pl.Element(bn), pl.Element(K)), lambda i, j: (l, src_row(j), 0)),
        ],
        out_specs=pl.BlockSpec((bm, bn), lambda i, j: (i, j)),
        out_shape=jax.ShapeDtypeStruct((M, n_out), F32),
        compiler_params=_params("parallel", "arbitrary"),
        name="inproj",
    )(x, w_in_t)


def _merge_body(yp_ref, ogp_ref, ogs_ref, wpu_ref, wdu_ref, ga_ref, gb_ref, o_ref, *, n_prompt_blocks):
    og = jnp.where(pl.program_id(1) < n_prompt_blocks, ogp_ref[...], ogs_ref[...])
    ya = _dot(yp_ref[...], wpu_ref[...])
    yb = _dot(og, wdu_ref[...])
    o_ref[...] = (jax.nn.sigmoid(ga_ref[...]) * ya + jax.nn.sigmoid(gb_ref[...]) * yb).astype(o_ref.dtype)


def _merge(yp, og_p, og_s, w_pool_up, w_delta_up, h_main, l, col_ga, col_gb, tb):
    M, DP = yp.shape
    DV = og_p.shape[1]
    D = w_pool_up.shape[-1]
    bn = _divisor(math.gcd(D, col_ga, col_gb), 2048, V7X_LANES)
    npb = og_p.shape[0] // tb
    assert og_s.shape[0] == tb and col_ga % bn == 0 and col_gb % bn == 0
    return pl.pallas_call(
        functools.partial(_merge_body, n_prompt_blocks=npb),
        grid=(D // bn, M // tb),
        in_specs=[
            pl.BlockSpec((tb, DP), lambda j, i: (i, 0)),
            pl.BlockSpec((tb, DV), lambda j, i: (jnp.minimum(i, npb - 1), 0)),
            pl.BlockSpec((tb, DV), lambda j, i: (0, 0), pipeline_mode=pl.Buffered(1)),
            pl.BlockSpec((None, DP, bn), lambda j, i: (l, 0, j), pipeline_mode=pl.Buffered(1)),
            pl.BlockSpec((None, DV, bn), lambda j, i: (l, 0, j), pipeline_mode=pl.Buffered(1)),
            pl.BlockSpec((tb, bn), lambda j, i: (i, col_ga // bn + j)),
            pl.BlockSpec((tb, bn), lambda j, i: (i, col_gb // bn + j)),
        ],
        out_specs=pl.BlockSpec((tb, bn), lambda j, i: (i, j)),
        out_shape=jax.ShapeDtypeStruct((M, D), BF16),
        compiler_params=_params("parallel", "parallel"),
        name="merge",
    )(yp, og_p, og_s, w_pool_up, w_delta_up, h_main, h_main)


def _outproj_body(a_ref, w_ref, x_ref, o_ref, *, alpha):
    a = a_ref[...]
    o_ref[...] = alpha * x_ref[...] + _dot(a, w_ref[...].astype(a.dtype))


def _outproj(a, w, x, l, alpha):
    M, K = a.shape
    N = w.shape[-1]
    bm = _divisor(M, 2176, V7X_BF16_ROWS)
    bn = _divisor(N, 256, V7X_LANES)
    return pl.pallas_call(
        functools.partial(_outproj_body, alpha=alpha),
        grid=(M // bm, N // bn),
        in_specs=[
            pl.BlockSpec((bm, K), lambda i, j: (i, 0), pipeline_mode=pl.Buffered(1)),
            pl.BlockSpec((None, K, bn), lambda i, j: (l, 0, j)),
            pl.BlockSpec((bm, bn), lambda i, j: (i, j)),
        ],
        out_specs=pl.BlockSpec((bm, bn), lambda i, j: (i, j)),
        out_shape=jax.ShapeDtypeStruct((M, N), F32),
        compiler_params=_params("parallel", "parallel"),
        name="outproj",
    )(a, w, x)


def _glu_body(x_ref, wg_ref, wu_ref, wd_ref, o_ref, wd16_ref, *, n_cast_steps):
    x = x_ref[...]
    g = _dot(x, wg_ref[...].astype(x.dtype))
    u = _dot(x, wu_ref[...].astype(x.dtype))
    o_ref[...] = (g * jax.nn.sigmoid(g) * u).astype(o_ref.dtype)

    @pl.when(pl.program_id(0) * pl.num_programs(1) + pl.program_id(1) < n_cast_steps)
    def _():
        wd16_ref[...] = wd_ref[...].astype(wd16_ref.dtype)


def _glu(x, w_gate, w_up, w_down, l):
    M, K = x.shape
    N = w_gate.shape[-1]
    F, D = w_down.shape[1], w_down.shape[2]
    bm = _divisor(M, 2176, V7X_BF16_ROWS)
    bn = _divisor(N, GLU_BLOCK_N, V7X_LANES)
    nj = N // bn
    steps = (M // bm) * nj
    slab = min(s for s in range(V7X_BF16_ROWS, F + 1, V7X_BF16_ROWS) if F % s == 0 and s >= 128 and F // s <= steps)
    n_cast_steps = F // slab

    def slab_index(i, j):
        return jnp.minimum(i * nj + j, n_cast_steps - 1)

    return pl.pallas_call(
        functools.partial(_glu_body, n_cast_steps=n_cast_steps),
        grid=(M // bm, nj),
        in_specs=[
            pl.BlockSpec((bm, K), lambda i, j: (i, 0), pipeline_mode=pl.Buffered(1)),
            pl.BlockSpec((None, K, bn), lambda i, j: (l, 0, j)),
            pl.BlockSpec((None, K, bn), lambda i, j: (l, 0, j)),
            pl.BlockSpec((None, slab, D), lambda i, j: (l, slab_index(i, j), 0)),
        ],
        out_specs=[
            pl.BlockSpec((bm, bn), lambda i, j: (i, j)),
            pl.BlockSpec((slab, D), lambda i, j: (slab_index(i, j), 0)),
        ],
        out_shape=[jax.ShapeDtypeStruct((M, N), BF16), jax.ShapeDtypeStruct((F, D), BF16)],
        compiler_params=_params("arbitrary", "arbitrary"),
        name="glu",
    )(x, w_gate, w_up, w_down)


def _down_body(a_ref, w_ref, x_ref, o_ref, *, alpha):
    @pl.when(pl.program_id(2) == 0)
    def _():
        o_ref[...] = alpha * x_ref[...]

    o_ref[...] += _dot(a_ref[...], w_ref[...])


def _down(a, w, x, alpha):
    M, K = a.shape
    N = w.shape[-1]
    bm = _divisor(M, 1088, V7X_BF16_ROWS)
    bn = _divisor(N, 512, V7X_LANES)
    tk = _divisor(K, 5504, V7X_LANES)
    return pl.pallas_call(
        functools.partial(_down_body, alpha=alpha),
        grid=(M // bm, N // bn, K // tk),
        in_specs=[
            pl.BlockSpec((bm, tk), lambda i, j, k: (i, k)),
            pl.BlockSpec((tk, bn), lambda i, j, k: (k, j)),
            pl.BlockSpec((bm, bn), lambda i, j, k: (i, j)),
        ],
        out_specs=pl.BlockSpec((bm, bn), lambda i, j, k: (i, j)),
        out_shape=jax.ShapeDtypeStruct((M, N), F32),
        compiler_params=_params("parallel", "parallel", "arbitrary"),
        name="down",
    )(a, w, x)


def _ln_body(r_ref, g_ref, b_ref, xf_ref, xb_ref):
    r = r_ref[...]
    mu = jnp.mean(r, axis=-1, keepdims=True)
    xc = r - mu
    var = jnp.mean(xc * xc, axis=-1, keepdims=True)
    y = xc * lax.rsqrt(var + LN_EPS) * g_ref[...] + b_ref[...]
    xf_ref[...] = y
    xb_ref[...] = y.astype(xb_ref.dtype)


def _layer_norm(r, g, b, l):
    M, D = r.shape
    bm = _divisor(M, 272, V7X_BF16_ROWS)
    return pl.pallas_call(
        _ln_body,
        grid=(M // bm,),
        in_specs=[
            pl.BlockSpec((bm, D), lambda i: (i, 0)),
            pl.BlockSpec((None, 1, D), lambda i: (l, 0, 0)),
            pl.BlockSpec((None, 1, D), lambda i: (l, 0, 0)),
        ],
        out_specs=[pl.BlockSpec((bm, D), lambda i: (i, 0)), pl.BlockSpec((bm, D), lambda i: (i, 0))],
        out_shape=[jax.ShapeDtypeStruct((M, D), F32), jax.ShapeDtypeStruct((M, D), BF16)],
        compiler_params=_params("parallel"),
        name="layer_norm",
    )(r, g, b)


def _pool_body(u_ref, st_ref, wg_ref, sc_ref, o_ref, ext_ref, *, tb, bs, ts, pb, gw, n_prompt_steps, steps_per_seq):
    step = pl.program_id(0)

    def finish(mixed):
        for gi, m in enumerate(mixed):
            cols = slice(gi * gw, (gi + 1) * gw)
            y = _dot(m.astype(BF16), wg_ref[gi])
            o_ref[:, cols] = (y * sc_ref[:, cols]).astype(o_ref.dtype)

    @pl.when(step < n_prompt_steps)
    def _():
        blk = step % steps_per_seq

        @pl.when(blk == 0)
        def _():
            ext_ref[0:POOL_HALO, :] = jnp.zeros((POOL_HALO, ext_ref.shape[1]), F32)

        ext_ref[POOL_HALO:POOL_HALO + tb, :] = u_ref[...]
        pos = blk * tb + lax.broadcasted_iota(jnp.int32, (tb, 1), 0)
        mixed = []
        for gi, w in enumerate(POOL_WINDOWS):
            cols = slice(gi * gw, (gi + 1) * gw)
            cur = ext_ref[POOL_HALO:POOL_HALO + tb, cols]
            win = cur
            for i in range(1, w):
                win = win + ext_ref[POOL_HALO - i:POOL_HALO - i + tb, cols]
            cnt = jnp.minimum(pos + 1, w).astype(F32)
            mixed.append(win / cnt - cur)
        finish(mixed)
        ext_ref[0:POOL_HALO, :] = ext_ref[tb:tb + POOL_HALO, :]

    @pl.when(step >= n_prompt_steps)
    def _():
        mixed = []
        for gi, w in enumerate(POOL_WINDOWS):
            cols = slice(gi * gw, (gi + 1) * gw)
            per_t = []
            for t in range(ts):
                acc = None
                for i in range(w):
                    e = pb + t - i
                    term = st_ref[e, :, cols] if e < pb else u_ref[(e - pb) * bs:(e - pb + 1) * bs, cols]
                    acc = term if acc is None else acc + term
                cnt = float(min(PAST_LEN + t + 1, w))
                per_t.append(acc / cnt - u_ref[t * bs:(t + 1) * bs, cols])
            mixed.append(jnp.concatenate(per_t, axis=0))
        finish(mixed)


def _pool(h_main, st_pool_tm, w_grp, scale, l, col_pool, tb, bs, ts, n_prompt_steps, steps_per_seq):
    M = h_main.shape[0]
    _, pb, _, DP = st_pool_tm.shape
    G, gw = w_grp.shape[1], w_grp.shape[2]
    assert G == len(POOL_WINDOWS) and pb == max(POOL_WINDOWS) - 1 and pb <= POOL_HALO and col_pool % DP == 0
    body = functools.partial(_pool_body, tb=tb, bs=bs, ts=ts, pb=pb, gw=gw,
                             n_prompt_steps=n_prompt_steps, steps_per_seq=steps_per_seq)
    return pl.pallas_call(
        body,
        grid=(M // tb,),
        in_specs=[
            pl.BlockSpec((tb, DP), lambda s: (s, col_pool // DP)),
            pl.BlockSpec((None, pb, bs, DP), lambda s: (l, 0, 0, 0)),
            pl.BlockSpec((None, G, gw, gw), lambda s: (l, 0, 0, 0)),
            pl.BlockSpec((None, 1, DP), lambda s: (l, 0, 0)),
        ],
        out_specs=pl.BlockSpec((tb, DP), lambda s: (s, 0)),
        out_shape=jax.ShapeDtypeStruct((M, DP), BF16),
        scratch_shapes=[pltpu.VMEM((POOL_HALO + tb, DP), F32)],
        compiler_params=_params("arbitrary"),
        name="pool_mixer",
    )(h_main, st_pool_tm, w_grp, scale)


def _prep_body(x_ref, hs_ref, stc_ref, cw_ref, al_ref, dtb_ref, qkv_ref, gb_ref, ext_ref, halo_ref, *,
               tb, bs, ts, cw, nh, hk, chunk, n_prompt_steps, steps_per_seq):
    step = pl.program_id(0)
    sec = pl.program_id(1)
    half = V7X_LANES // 2

    def emit(conv_head):
        @pl.when(sec < 2)
        def _():
            qscale = jnp.where(sec == 0, hk ** -0.5, 1.0).astype(F32)
            for h in range(nh):
                y = conv_head(h)
                y = y * jax.nn.sigmoid(y)
                ss = jnp.sum(y * y, axis=-1, keepdims=True)
                qkv_ref[h] = y * (lax.rsqrt(ss + RMS_EPS) * qscale)

        @pl.when(sec == 2)
        def _():
            for h in range(nh):
                y = conv_head(h)
                qkv_ref[h] = y * jax.nn.sigmoid(y)

    def emit_gates(gcum, beta):
        for h in range(nh):
            gcol = jnp.broadcast_to(gcum[:, nh + h:nh + h + 1], (tb, half))
            bcol = jnp.broadcast_to(beta[:, h:h + 1], (tb, half))
            gb_ref[h] = jnp.concatenate([gcol, bcol], axis=1)

    def gate_values():
        hs = hs_ref[...]
        beta = jax.nn.sigmoid(hs)
        g = -jnp.exp(al_ref[...]) * _softplus(hs + dtb_ref[...])
        return g, beta

    @pl.when(step < n_prompt_steps)
    def _():
        blk = step % steps_per_seq

        @pl.when(blk == 0)
        def _():
            ext_ref[0:CONV_HALO, :] = jnp.zeros((CONV_HALO, ext_ref.shape[1]), F32)

        @pl.when(blk != 0)
        def _():
            ext_ref[0:CONV_HALO, :] = halo_ref[sec]

        ext_ref[CONV_HALO:CONV_HALO + tb, :] = x_ref[...]
        halo_ref[sec] = ext_ref[tb:tb + CONV_HALO, :]
        base = CONV_HALO - (cw - 1)

        def conv_head(h):
            cols = slice(h * hk, (h + 1) * hk)
            y = ext_ref[base:base + tb, cols] * cw_ref[0:1, cols]
            for i in range(1, cw):
                y = y + ext_ref[base + i:base + i + tb, cols] * cw_ref[i:i + 1, cols]
            return y

        emit(conv_head)

        @pl.when(sec == 0)
        def _():
            g, beta = gate_values()
            t_in = lax.broadcasted_iota(jnp.int32, g.shape, 0) % chunk
            d = 1
            while d < chunk:
                g = g + jnp.where(t_in >= d, pltpu.roll(g, d, 0), 0.0)
                d *= 2
            emit_gates(g, beta)

    @pl.when(step >= n_prompt_steps)
    def _():
        def ext_slab(j, cols):
            return stc_ref[j, :, cols] if j < cw - 1 else x_ref[(j - (cw - 1)) * bs:(j - (cw - 2)) * bs, cols]

        def conv_head(h):
            cols = slice(h * hk, (h + 1) * hk)
            per_t = []
            for t in range(ts):
                y = ext_slab(t, cols) * cw_ref[0:1, cols]
                for i in range(1, cw):
                    y = y + ext_slab(t + i, cols) * cw_ref[i:i + 1, cols]
                per_t.append(y)
            return jnp.concatenate(per_t, axis=0)

        emit(conv_head)

        @pl.when(sec == 0)
        def _():
            g, beta = gate_values()
            slabs = [g[0:bs]]
            for t in range(1, ts):
                slabs.append(slabs[-1] + g[t * bs:(t + 1) * bs])
            emit_gates(jnp.concatenate(slabs, axis=0), beta)


def _prep(h_main, st_conv_tm, conv_w, alog_row, dtb_row, l, col_qkv, col_small, tb, bs, ts, nh, hk,
          n_prompt_steps, steps_per_seq):
    M = h_main.shape[0]
    cw = conv_w.shape[1]
    sw = nh * hk
    assert cw - 1 <= CONV_HALO and col_qkv % sw == 0 and 2 * nh <= V7X_LANES and col_small % V7X_LANES == 0
    body = functools.partial(_prep_body, tb=tb, bs=bs, ts=ts, cw=cw, nh=nh, hk=hk, chunk=DELTA_CHUNK,
                             n_prompt_steps=n_prompt_steps, steps_per_seq=steps_per_seq)
    return pl.pallas_call(
        body,
        grid=(M // tb, 3),
        in_specs=[
            pl.BlockSpec((tb, sw), lambda s, c: (s, col_qkv // sw + c)),
            pl.BlockSpec((tb, V7X_LANES), lambda s, c: (s, col_small // V7X_LANES)),
            pl.BlockSpec((None, cw - 1, bs, sw), lambda s, c: (l, 0, 0, c)),
            pl.BlockSpec((None, cw, sw), lambda s, c: (l, 0, c)),
            pl.BlockSpec((None, 1, V7X_LANES), lambda s, c: (l, 0, 0)),
            pl.BlockSpec((None, 1, V7X_LANES), lambda s, c: (l, 0, 0)),
        ],
        out_specs=[
            pl.BlockSpec((nh, tb, hk), lambda s, c: (c, s, 0)),
            pl.BlockSpec((nh, tb, V7X_LANES), lambda s, c: (0, s, 0)),
        ],
        out_shape=[jax.ShapeDtypeStruct((3 * nh, M, hk), F32), jax.ShapeDtypeStruct((nh, M, V7X_LANES), F32)],
        scratch_shapes=[pltpu.VMEM((CONV_HALO + tb, sw), F32), pltpu.VMEM((3, CONV_HALO, sw), F32)],
        compiler_params=_params("arbitrary", "arbitrary"),
        name="delta_prep",
    )(h_main, h_main, st_conv_tm, conv_w, alog_row, dtb_row)


def _gated_norm(o, z, nw):
    o = o * lax.rsqrt(jnp.mean(o * o, axis=-1, keepdims=True) + RMS_EPS) * nw
    return o * (z * jax.nn.sigmoid(z))


def _delta_prompt_body(q_ref, k_ref, v_ref, gb_ref, z_ref, nw_ref, og_ref, so_ref, s_ref, *, hg, c, hk, hv):
    ci = pl.program_id(2)
    half = V7X_LANES // 2

    @pl.when(ci == 0)
    def _():
        s_ref[...] = jnp.zeros(s_ref.shape, F32)

    ti = lax.broadcasted_iota(jnp.int32, (c, c), 0)
    si = lax.broadcasted_iota(jnp.int32, (c, c), 1)
    incl = ti >= si
    strict = ti > si
    heads = range(hg)
    q = [q_ref[j] for j in heads]
    k = [k_ref[j] for j in heads]
    gbv = [gb_ref[j] for j in heads]
    g_col = [x[:, 0:1] for x in gbv]
    b_col = [x[:, half:half + 1] for x in gbv]
    decay = [jnp.where(incl, jnp.exp(jnp.where(incl, x[:, 0:c] - jnp.concatenate([x, x], axis=0).T[0:c, 0:c], 0.0)), 0.0)
             for x in gbv]
    kb = [k[j] * b_col[j] for j in heads]
    k16 = [x.astype(BF16) for x in k]
    a = [jnp.where(strict, _dot_t(kb[j].astype(BF16), k16[j]) * decay[j], 0.0) for j in heads]
    qk = [_dot_t(q[j].astype(BF16), k16[j]) * decay[j] for j in heads]
    n = [-x for x in a]
    p = [_dot(x.astype(BF16), x.astype(BF16)) for x in a]
    span = 4
    while span < c:
        both = [_dot(jnp.concatenate([n[j], p[j]], axis=0).astype(BF16), p[j].astype(BF16)) for j in heads]
        n = [n[j] + p[j] + both[j][0:c] for j in heads]
        p = [both[j][c:2 * c] for j in heads]
        span *= 2
    n = [n[j] + p[j] + _dot(n[j].astype(BF16), p[j].astype(BF16)) for j in heads]
    eg = [jnp.exp(x) for x in g_col]
    rhs = [jnp.concatenate([kb[j] * eg[j], v_ref[j] * b_col[j]], axis=1) for j in heads]
    tr = [rhs[j] + _dot(n[j].astype(BF16), rhs[j].astype(BF16)) for j in heads]
    g_last = [x[c - 1:c, :] for x in g_col]
    k_tail = [(k[j] * jnp.exp(g_last[j] - g_col[j])).astype(BF16) for j in heads]
    s0 = [s_ref[j] for j in heads]
    ps = [_dot(jnp.concatenate([tr[j][:, 0:hk], q[j] * eg[j]], axis=0).astype(BF16), s0[j].astype(BF16)) for j in heads]
    u = [(tr[j][:, hk:hk + hv] - ps[j][0:c]).astype(BF16) for j in heads]
    for j in heads:
        s_ref[j] = jnp.exp(g_last[j]) * s0[j] + _tdot(k_tail[j], u[j])
    o = [ps[j][c:2 * c] + _dot(qk[j].astype(BF16), u[j]) for j in heads]
    for j in heads:
        cols = slice(j * hv, (j + 1) * hv)
        og_ref[:, cols] = _gated_norm(o[j], z_ref[:, cols], nw_ref[...]).astype(og_ref.dtype)

    @pl.when(ci == pl.num_programs(2) - 1)
    def _():
        so_ref[...] = s_ref[...]


def _delta_prompt(qkv, gb, h_main, nw, l, col_z, nb, t, nh, hk, hv, hg):
    c = DELTA_CHUNK
    assert t % c == 0 and nh % hg == 0 and col_z % (hg * hv) == 0 and hk == V7X_LANES and c <= V7X_LANES // 2
    ncs = t // c
    ngr = nh // hg
    body = functools.partial(_delta_prompt_body, hg=hg, c=c, hk=hk, hv=hv)

    def head_spec(sec):
        return pl.BlockSpec((hg, c, hk), lambda b, g, ci: (sec * ngr + g, b * ncs + ci, 0))

    return pl.pallas_call(
        body,
        grid=(nb, ngr, ncs),
        in_specs=[
            head_spec(0), head_spec(1), head_spec(2),
            pl.BlockSpec((hg, c, V7X_LANES), lambda b, g, ci: (g, b * ncs + ci, 0)),
            pl.BlockSpec((c, hg * hv), lambda b, g, ci: (b * ncs + ci, col_z // (hg * hv) + g)),
            pl.BlockSpec((None, 1, hv), lambda b, g, ci: (l, 0, 0)),
        ],
        out_specs=[
            pl.BlockSpec((c, hg * hv), lambda b, g, ci: (b * ncs + ci, g)),
            pl.BlockSpec((None, hg, hk, hv), lambda b, g, ci: (b, g, 0, 0)),
        ],
        out_shape=[jax.ShapeDtypeStruct((nb * t, nh * hv), BF16), jax.ShapeDtypeStruct((nb, nh, hk, hv), F32)],
        scratch_shapes=[pltpu.VMEM((hg, hk, hv), F32)],
        compiler_params=_params("parallel", "parallel", "arbitrary"),
        name="delta_prompt",
    )(qkv, qkv, qkv, gb, h_main, nw)


def _delta_sample_body(q_ref, k_ref, v_ref, gb_ref, z_ref, nw_ref, s_ref, og_ref, so_ref,
                       kq_scr, p_scr, u_scr, kt_scr, e_scr, *, bs, ts, bb, hk, hv):
    b0 = pl.multiple_of(pl.program_id(1) * bb, bb)
    half = V7X_LANES // 2
    pad = V7X_SUBLANES - ts

    def rows(ref, t):
        return ref[pl.ds(t * bs + b0, bb), :]

    ks = [rows(k_ref, t) for t in range(ts)]
    qs = [rows(q_ref, t) for t in range(ts)]
    for t in range(ts):
        kq_scr[t * bb:(t + 1) * bb, :] = ks[t]
        kq_scr[(ts + t) * bb:(ts + t + 1) * bb, :] = qs[t]

    def state_products(g, carry):
        seqs = [g * SAMPLE_GROUP + i for i in range(SAMPLE_GROUP)]
        kq = [kq_scr[pl.ds(b, 2 * ts, stride=bb), :].astype(BF16) for b in seqs]
        s16 = [s_ref[b].astype(BF16) for b in seqs]
        prod = [_dot(kq[i], s16[i]) for i in range(SAMPLE_GROUP)]
        for i, b in enumerate(seqs):
            p_scr[pl.ds(b, 2 * ts, stride=bb), :] = prod[i]
        return carry

    lax.fori_loop(0, bb // SAMPLE_GROUP, state_products, 0)

    gs = [rows(gb_ref, t)[:, 0:1] for t in range(ts)]
    betas = [rows(gb_ref, t)[:, half:half + 1] for t in range(ts)]
    us = []
    for t in range(ts):
        u = betas[t] * (rows(v_ref, t) - jnp.exp(gs[t]) * p_scr[t * bb:(t + 1) * bb, :])
        for s in range(t):
            kk = jnp.sum(ks[t] * ks[s], axis=-1, keepdims=True)
            u = u - (betas[t] * kk * jnp.exp(gs[t] - gs[s])) * us[s]
        us.append(u)
    for t in range(ts):
        o = jnp.exp(gs[t]) * p_scr[(ts + t) * bb:(ts + t + 1) * bb, :]
        for s in range(t + 1):
            qk = jnp.sum(qs[t] * ks[s], axis=-1, keepdims=True)
            o = o + (qk * jnp.exp(gs[t] - gs[s])) * us[s]
        og_ref[pl.ds(t * bs + b0, bb), :] = _gated_norm(o, rows(z_ref, t), nw_ref[...]).astype(og_ref.dtype)
        u_scr[t * bb:(t + 1) * bb, :] = us[t]
        kt_scr[t * bb:(t + 1) * bb, :] = ks[t] * jnp.exp(gs[ts - 1] - gs[t])
    u_scr[ts * bb:(ts + pad) * bb, :] = jnp.zeros((pad * bb, hv), F32)
    kt_scr[ts * bb:(ts + pad) * bb, :] = jnp.zeros((pad * bb, hk), F32)
    e_scr[...] = jnp.broadcast_to(jnp.exp(gs[ts - 1]), (bb, hv))

    def state_update(g, carry):
        seqs = [g * SAMPLE_GROUP + i for i in range(SAMPLE_GROUP)]
        ub = [u_scr[pl.ds(b, V7X_SUBLANES, stride=bb), :].astype(BF16) for b in seqs]
        kb = [kt_scr[pl.ds(b, V7X_SUBLANES, stride=bb), :].astype(BF16) for b in seqs]
        upd = [_tdot(kb[i], ub[i]) for i in range(SAMPLE_GROUP)]
        for i, b in enumerate(seqs):
            so_ref[b] = e_scr[pl.ds(b, 1), :] * s_ref[b] + upd[i]
        return carry

    lax.fori_loop(0, bb // SAMPLE_GROUP, state_update, 0)


def _delta_sample(qkv, gb, h_main, nw, state, l, col_z, row0, bs, ts, nh, hk, hv):
    tbs = bs * ts
    bb = _divisor(bs, 64, V7X_BF16_ROWS)
    assert row0 % tbs == 0 and col_z % hv == 0 and 2 * ts <= V7X_SUBLANES and bb % SAMPLE_GROUP == 0
    body = functools.partial(_delta_sample_body, bs=bs, ts=ts, bb=bb, hk=hk, hv=hv)

    def head_spec(sec):
        return pl.BlockSpec((None, tbs, hk), lambda h, g: (sec * nh + h, row0 // tbs, 0))

    return pl.pallas_call(
        body,
        grid=(nh, bs // bb),
        in_specs=[
            head_spec(0), head_spec(1), head_spec(2),
            pl.BlockSpec((None, tbs, V7X_LANES), lambda h, g: (h, row0 // tbs, 0)),
            pl.BlockSpec((tbs, hv), lambda h, g: (row0 // tbs, col_z // hv + h)),
            pl.BlockSpec((None, 1, hv), lambda h, g: (l, 0, 0)),
            pl.BlockSpec((None, bb, None, hk, hv), lambda h, g: (l, g, h, 0, 0)),
        ],
        out_specs=[
            pl.BlockSpec((tbs, hv), lambda h, g: (0, h)),
            pl.BlockSpec((bb, None, hk, hv), lambda h, g: (g, h, 0, 0)),
        ],
        out_shape=[jax.ShapeDtypeStruct((tbs, nh * hv), BF16), jax.ShapeDtypeStruct((bs, nh, hk, hv), F32)],
        scratch_shapes=[
            pltpu.VMEM((2 * ts * bb, hk), F32),
            pltpu.VMEM((2 * ts * bb, hv), F32),
            pltpu.VMEM((V7X_SUBLANES * bb, hv), F32),
            pltpu.VMEM((V7X_SUBLANES * bb, hk), F32),
            pltpu.VMEM((bb, hv), F32),
        ],
        compiler_params=_params("parallel", "arbitrary"),
        name="delta_sample",
    )(qkv, qkv, qkv, gb, h_main, nw, state)


def kernel(x_prompt, x_sample, state_pool, state_conv, state_delta, w_in, w_pool_grp, pool_scale, w_pool_up,
           conv_w, a_log, dt_bias, o_norm_w, w_delta_up, w_out, ln1_g, ln1_b, w_gate, w_up, w_down, ln2_g, ln2_b):
    nb, t, d = x_prompt.shape
    bs, ts, _ = x_sample.shape
    depth = w_in.shape[0]
    pb, dp = state_pool.shape[2], state_pool.shape[3]
    cwm1 = state_conv.shape[2]
    assert t >= pb and t >= cwm1
    dqkv = state_conv.shape[3]
    nh, hk, hv = state_delta.shape[2], state_delta.shape[3], state_delta.shape[4]
    dv = nh * hv
    dff = w_gate.shape[-1]
    assert dqkv == 2 * nh * hk + dv and hk == hv
    mp, ms = nb * t, bs * ts
    m = mp + ms
    tb = ms
    assert t % tb == 0 and tb % DELTA_CHUNK == 0
    steps_per_seq = t // tb
    n_prompt_steps = mp // tb
    alpha = float((2 * depth) ** 0.25)

    off_qkv = dp
    off_z = off_qkv + dqkv
    off_beta = off_z + dv
    off_ga = off_beta + 2 * nh
    off_gb = off_ga + d
    assert w_in.shape[-1] == off_gb + d
    col_qkv, col_z, col_pool = 0, dqkv, dqkv + dv
    col_ga = col_pool + dp
    col_gb = col_ga + d
    col_small = col_gb + d
    segments = ((off_qkv, col_qkv, dqkv), (off_z, col_z, dv), (0, col_pool, dp), (off_ga, col_ga, 2 * d),
                (off_beta, col_small, INPROJ_BLOCK_N))
    assert off_beta + INPROJ_BLOCK_N <= w_in.shape[-1]
    w_in_t = jnp.swapaxes(w_in, 1, 2)
    w_grp16 = w_pool_grp.astype(BF16)
    w_pu16 = w_pool_up.astype(BF16)
    w_du16 = w_delta_up.astype(BF16)

    lane_pad = ((0, 0), (nh, V7X_LANES - 2 * nh))
    alog_row = jnp.pad(a_log, lane_pad)[:, None, :]
    dtb_row = jnp.pad(dt_bias, lane_pad)[:, None, :]
    scale_row = pool_scale[:, None, :]
    nw_row = o_norm_w[:, None, :]
    ln1_g, ln1_b, ln2_g, ln2_b = (a[:, None, :] for a in (ln1_g, ln1_b, ln2_g, ln2_b))
    st_pool_tm = jnp.swapaxes(state_pool, 1, 2)
    st_conv_tm = jnp.swapaxes(state_conv, 1, 2)

    x = jnp.concatenate([x_prompt.reshape(mp, d), jnp.swapaxes(x_sample, 0, 1).reshape(ms, d)], axis=0)
    x16 = x.astype(BF16)

    pool_p, conv_p, delta_p, pool_s, conv_s, delta_s = [], [], [], [], [], []
    for l in range(depth):
        h_main = _inproj(x16, w_in_t, segments, l)
        yp = _pool(h_main, st_pool_tm, w_grp16, scale_row, l, col_pool, tb, bs, ts, n_prompt_steps, steps_per_seq)
        qkv, gb = _prep(h_main, st_conv_tm, conv_w, alog_row, dtb_row, l, col_qkv, col_small, tb, bs, ts, nh, hk,
                        n_prompt_steps, steps_per_seq)
        og_p, s_p = _delta_prompt(qkv, gb, h_main, nw_row, l, col_z, nb, t, nh, hk, hv, hg=min(nh, 16))
        og_s, s_s = _delta_sample(qkv, gb, h_main, nw_row, state_delta, l, col_z, mp, bs, ts, nh, hk, hv)
        merged = _merge(yp, og_p, og_s, w_pu16, w_du16, h_main, l, col_ga, col_gb, tb)
        r1 = _outproj(merged, w_out, x, l, alpha)
        x1, x1_16 = _layer_norm(r1, ln1_g, ln1_b, l)
        act, w_down16 = _glu(x1_16, w_gate, w_up, w_down, l)
        r2 = _down(act, w_down16, x1, alpha)
        x, x16 = _layer_norm(r2, ln2_g, ln2_b, l)

        def tail_rows(col, width, keep):
            return jnp.stack([h_main[(b + 1) * t - keep:(b + 1) * t, col:col + width] for b in range(nb)])

        def sample_rows(col, width):
            return jnp.swapaxes(h_main[mp:, col:col + width].reshape(ts, bs, width), 0, 1)

        pool_p.append(tail_rows(col_pool, dp, pb))
        conv_p.append(tail_rows(col_qkv, dqkv, cwm1))
        pool_s.append(jnp.concatenate([state_pool[l], sample_rows(col_pool, dp)], axis=1)[:, -pb:])
        conv_s.append(jnp.concatenate([state_conv[l], sample_rows(col_qkv, dqkv)], axis=1)[:, -cwm1:])
        delta_p.append(s_p)
        delta_s.append(s_s)

    y_prompt = x[:mp].reshape(nb, t, d)
    y_sample = jnp.swapaxes(x[mp:].reshape(ts, bs, d), 0, 1)
    return (y_prompt, y_sample, jnp.stack(pool_p), jnp.stack(conv_p), jnp.stack(delta_p),
            jnp.stack(pool_s), jnp.stack(conv_s), jnp.stack(delta_s))
```

```python
import functools
import math

import jax
import jax.numpy as jnp
from jax import lax
from jax.experimental import pallas as pl
from jax.experimental.pallas import tpu as pltpu

F32 = jnp.float32
BF16 = jnp.bfloat16

POOL_WINDOWS = (2, 4, 8, 16)
DELTA_CHUNK = 64
PAST_LEN = 16384
LN_EPS = 1e-5
RMS_EPS = 1e-6

V7X_LANES = 128
V7X_SUBLANES = 8
V7X_BF16_ROWS = 16
V7X_VMEM_LIMIT_BYTES = 56 * 1024 * 1024
POOL_HALO = 16
CONV_HALO = V7X_SUBLANES
GLU_BLOCK_N = 256
INPROJ_BLOCK_N = 512
SAMPLE_GROUP = 8


def _params(*semantics):
    return pltpu.CompilerParams(dimension_semantics=semantics, vmem_limit_bytes=V7X_VMEM_LIMIT_BYTES)


def _divisor(n, target, mult):
    best = None
    for d in range(mult, min(n, target) + 1, mult):
        if n % d == 0:
            best = d
    assert best is not None, (n, target, mult)
    return best


def _dot(a, b):
    return jnp.dot(a, b, preferred_element_type=F32)


def _dot_t(a, b):
    return lax.dot_general(a, b, (((1,), (1,)), ((), ())), preferred_element_type=F32)


def _tdot(a, b):
    return lax.dot_general(a, b, (((0,), (0,)), ((), ())), preferred_element_type=F32)


def _softplus(x):
    return jnp.maximum(x, 0.0) + jnp.log1p(jnp.exp(-jnp.abs(x)))


def _inproj_body(x_ref, w_ref, o_ref):
    x = x_ref[...]
    o_ref[...] = _dot(x, w_ref[0].astype(x.dtype).T)


def _inproj(x, w_in_t, segments, l):
    M, K = x.shape
    n_out = sum(width for _, _, width in segments)
    bm = _divisor(M, 2176, V7X_BF16_ROWS)
    bn = INPROJ_BLOCK_N
    assert all(src % V7X_SUBLANES == 0 and dst % bn == 0 and width % bn == 0 for src, dst, width in segments)

    def src_row(j):
        col = j * bn
        row = 0
        for src, dst, width in segments:
            row = jnp.where((col >= dst) & (col < dst + width), src + col - dst, row)
        return pl.multiple_of(row, V7X_SUBLANES)

    return pl.pallas_call(
        _inproj_body,
        grid=(M // bm, n_out // bn),
        in_specs=[
            pl.BlockSpec((bm, K), lambda i, j: (i, 0), pipeline_mode=pl.Buffered(1)),
            pl.BlockSpec((pl.Element(1), pl.Element(bn), pl.Element(K)), lambda i, j: (l, src_row(j), 0)),
        ],
        out_specs=pl.BlockSpec((bm, bn), lambda i, j: (i, j)),
        out_shape=jax.ShapeDtypeStruct((M, n_out), F32),
        compiler_params=_params("parallel", "arbitrary"),
        name="inproj",
    )(x, w_in_t)


def _merge_body(yp_ref, ogp_ref, ogs_ref, wpu_ref, wdu_ref, ga_ref, gb_ref, o_ref, *, n_prompt_blocks):
    og = jnp.where(pl.program_id(1) < n_prompt_blocks, ogp_ref[...], ogs_ref[...])
    ya = _dot(yp_ref[...], wpu_ref[...])
    yb = _dot(og, wdu_ref[...])
    o_ref[...] = (jax.nn.sigmoid(ga_ref[...]) * ya + jax.nn.sigmoid(gb_ref[...]) * yb).astype(o_ref.dtype)


def _merge(yp, og_p, og_s, w_pool_up, w_delta_up, h_main, l, col_ga, col_gb, tb):
    M, DP = yp.shape
    DV = og_p.shape[1]
    D = w_pool_up.shape[-1]
    bn = _divisor(math.gcd(D, col_ga, col_gb), 2048, V7X_LANES)
    npb = og_p.shape[0] // tb
    assert og_s.shape[0] == tb and col_ga % bn == 0 and col_gb % bn == 0
    return pl.pallas_call(
        functools.partial(_merge_body, n_prompt_blocks=npb),
        grid=(D // bn, M // tb),
        in_specs=[
            pl.BlockSpec((tb, DP), lambda j, i: (i, 0)),
            pl.BlockSpec((tb, DV), lambda j, i: (jnp.minimum(i, npb - 1), 0)),
            pl.BlockSpec((tb, DV), lambda j, i: (0, 0), pipeline_mode=pl.Buffered(1)),
            pl.BlockSpec((None, DP, bn), lambda j, i: (l, 0, j), pipeline_mode=pl.Buffered(1)),
            pl.BlockSpec((None, DV, bn), lambda j, i: (l, 0, j), pipeline_mode=pl.Buffered(1)),
            pl.BlockSpec((tb, bn), lambda j, i: (i, col_ga // bn + j)),
            pl.BlockSpec((tb, bn), lambda j, i: (i, col_gb // bn + j)),
        ],
        out_specs=pl.BlockSpec((tb, bn), lambda j, i: (i, j)),
        out_shape=jax.ShapeDtypeStruct((M, D), BF16),
        compiler_params=_params("parallel", "parallel"),
        name="merge",
    )(yp, og_p, og_s, w_pool_up, w_delta_up, h_main, h_main)


def _outproj_body(a_ref, w_ref, x_ref, o_ref, *, alpha):
    a = a_ref[...]
    o_ref[...] = alpha * x_ref[...] + _dot(a, w_ref[...].astype(a.dtype))


def _outproj(a, w, x, l, alpha):
    M, K = a.shape
    N = w.shape[-1]
    bm = _divisor(M, 1088, V7X_BF16_ROWS)
    bn = _divisor(N, 512, V7X_LANES)
    return pl.pallas_call(
        functools.partial(_outproj_body, alpha=alpha),
        grid=(M // bm, N // bn),
        in_specs=[
            pl.BlockSpec((bm, K), lambda i, j: (i, 0)),
            pl.BlockSpec((None, K, bn), lambda i, j: (l, 0, j)),
            pl.BlockSpec((bm, bn), lambda i, j: (i, j)),
        ],
        out_specs=pl.BlockSpec((bm, bn), lambda i, j: (i, j)),
        out_shape=jax.ShapeDtypeStruct((M, N), F32),
        compiler_params=_params("parallel", "parallel"),
        name="outproj",
    )(a, w, x)


def _glu_body(x_ref, wg_ref, wu_ref, wd_ref, o_ref, wd16_ref, *, n_cast_steps):
    x = x_ref[...]
    g = _dot(x, wg_ref[...].astype(x.dtype))
    u = _dot(x, wu_ref[...].astype(x.dtype))
    o_ref[...] = (g * jax.nn.sigmoid(g) * u).astype(o_ref.dtype)

    @pl.when(pl.program_id(0) * pl.num_programs(1) + pl.program_id(1) < n_cast_steps)
    def _():
        wd16_ref[...] = wd_ref[...].astype(wd16_ref.dtype)


def _glu(x, w_gate, w_up, w_down, l):
    M, K = x.shape
    N = w_gate.shape[-1]
    F, D = w_down.shape[1], w_down.shape[2]
    bm = _divisor(M, 2176, V7X_BF16_ROWS)
    bn = _divisor(N, GLU_BLOCK_N, V7X_LANES)
    nj = N // bn
    steps = (M // bm) * nj
    slab = min(s for s in range(V7X_BF16_ROWS, F + 1, V7X_BF16_ROWS) if F % s == 0 and s >= 128 and F // s <= steps)
    n_cast_steps = F // slab

    def slab_index(i, j):
        return jnp.minimum(i * nj + j, n_cast_steps - 1)

    return pl.pallas_call(
        functools.partial(_glu_body, n_cast_steps=n_cast_steps),
        grid=(M // bm, nj),
        in_specs=[
            pl.BlockSpec((bm, K), lambda i, j: (i, 0), pipeline_mode=pl.Buffered(1)),
            pl.BlockSpec((None, K, bn), lambda i, j: (l, 0, j)),
            pl.BlockSpec((None, K, bn), lambda i, j: (l, 0, j)),
            pl.BlockSpec((None, slab, D), lambda i, j: (l, slab_index(i, j), 0)),
        ],
        out_specs=[
            pl.BlockSpec((bm, bn), lambda i, j: (i, j)),
            pl.BlockSpec((slab, D), lambda i, j: (slab_index(i, j), 0)),
        ],
        out_shape=[jax.ShapeDtypeStruct((M, N), BF16), jax.ShapeDtypeStruct((F, D), BF16)],
        compiler_params=_params("arbitrary", "arbitrary"),
        name="glu",
    )(x, w_gate, w_up, w_down)


def _down_body(a_ref, w_ref, x_ref, o_ref, *, alpha):
    @pl.when(pl.program_id(2) == 0)
    def _():
        o_ref[...] = alpha * x_ref[...]

    o_ref[...] += _dot(a_ref[...], w_ref[...])


def _down(a, w, x, alpha):
    M, K = a.shape
    N = w.shape[-1]
    bm = _divisor(M, 1088, V7X_BF16_ROWS)
    bn = _divisor(N, 512, V7X_LANES)
    tk = _divisor(K, 5504, V7X_LANES)
    return pl.pallas_call(
        functools.partial(_down_body, alpha=alpha),
        grid=(M // bm, N // bn, K // tk),
        in_specs=[
            pl.BlockSpec((bm, tk), lambda i, j, k: (i, k)),
            pl.BlockSpec((tk, bn), lambda i, j, k: (k, j)),
            pl.BlockSpec((bm, bn), lambda i, j, k: (i, j)),
        ],
        out_specs=pl.BlockSpec((bm, bn), lambda i, j, k: (i, j)),
        out_shape=jax.ShapeDtypeStruct((M, N), F32),
        compiler_params=_params("parallel", "parallel", "arbitrary"),
        name="down",
    )(a, w, x)


def _ln_body(r_ref, g_ref, b_ref, xf_ref, xb_ref):
    r = r_ref[...]
    mu = jnp.mean(r, axis=-1, keepdims=True)
    xc = r - mu
    var = jnp.mean(xc * xc, axis=-1, keepdims=True)
    y = xc * lax.rsqrt(var + LN_EPS) * g_ref[...] + b_ref[...]
    xf_ref[...] = y
    xb_ref[...] = y.astype(xb_ref.dtype)


def _ln_split_body(r_ref, g_ref, b_ref, yp_ref, ys_ref, *, n_prompt_blocks):
    r = r_ref[...]
    mu = jnp.mean(r, axis=-1, keepdims=True)
    xc = r - mu
    var = jnp.mean(xc * xc, axis=-1, keepdims=True)
    y = xc * lax.rsqrt(var + LN_EPS) * g_ref[...] + b_ref[...]

    @pl.when(pl.program_id(0) < n_prompt_blocks)
    def _():
        yp_ref[...] = y

    @pl.when(pl.program_id(0) >= n_prompt_blocks)
    def _():
        ys_ref[...] = y


def _layer_norm_split(r, g, b, l, mp):
    M, D = r.shape
    ms = M - mp
    bm = _divisor(math.gcd(mp, ms), 272, V7X_SUBLANES)
    npb = mp // bm
    return pl.pallas_call(
        functools.partial(_ln_split_body, n_prompt_blocks=npb),
        grid=(M // bm,),
        in_specs=[
            pl.BlockSpec((bm, D), lambda i: (i, 0)),
            pl.BlockSpec((None, 1, D), lambda i: (l, 0, 0)),
            pl.BlockSpec((None, 1, D), lambda i: (l, 0, 0)),
        ],
        out_specs=[
            pl.BlockSpec((bm, D), lambda i: (jnp.minimum(i, npb - 1), 0)),
            pl.BlockSpec((bm, D), lambda i: (jnp.maximum(i - npb, 0), 0)),
        ],
        out_shape=[jax.ShapeDtypeStruct((mp, D), F32), jax.ShapeDtypeStruct((ms, D), F32)],
        compiler_params=_params("arbitrary"),
        name="layer_norm_split",
    )(r, g, b)


def _layer_norm(r, g, b, l):
    M, D = r.shape
    bm = _divisor(M, 272, V7X_BF16_ROWS)
    return pl.pallas_call(
        _ln_body,
        grid=(M // bm,),
        in_specs=[
            pl.BlockSpec((bm, D), lambda i: (i, 0)),
            pl.BlockSpec((None, 1, D), lambda i: (l, 0, 0)),
            pl.BlockSpec((None, 1, D), lambda i: (l, 0, 0)),
        ],
        out_specs=[pl.BlockSpec((bm, D), lambda i: (i, 0)), pl.BlockSpec((bm, D), lambda i: (i, 0))],
        out_shape=[jax.ShapeDtypeStruct((M, D), F32), jax.ShapeDtypeStruct((M, D), BF16)],
        compiler_params=_params("parallel"),
        name="layer_norm",
    )(r, g, b)


def _pool_body(u_ref, st_ref, wg_ref, sc_ref, o_ref, ext_ref, *, tb, bs, ts, pb, gw, n_prompt_steps, steps_per_seq):
    step = pl.program_id(0)

    def finish(mixed):
        for gi, m in enumerate(mixed):
            cols = slice(gi * gw, (gi + 1) * gw)
            y = _dot(m.astype(BF16), wg_ref[gi])
            o_ref[:, cols] = (y * sc_ref[:, cols]).astype(o_ref.dtype)

    @pl.when(step < n_prompt_steps)
    def _():
        blk = step % steps_per_seq

        @pl.when(blk == 0)
        def _():
            ext_ref[0:POOL_HALO, :] = jnp.zeros((POOL_HALO, ext_ref.shape[1]), F32)

        ext_ref[POOL_HALO:POOL_HALO + tb, :] = u_ref[...]
        pos = blk * tb + lax.broadcasted_iota(jnp.int32, (tb, 1), 0)
        mixed = []
        for gi, w in enumerate(POOL_WINDOWS):
            cols = slice(gi * gw, (gi + 1) * gw)
            cur = ext_ref[POOL_HALO:POOL_HALO + tb, cols]
            win = cur
            for i in range(1, w):
                win = win + ext_ref[POOL_HALO - i:POOL_HALO - i + tb, cols]
            cnt = jnp.minimum(pos + 1, w).astype(F32)
            mixed.append(win / cnt - cur)
        finish(mixed)
        ext_ref[0:POOL_HALO, :] = ext_ref[tb:tb + POOL_HALO, :]

    @pl.when(step >= n_prompt_steps)
    def _():
        mixed = []
        for gi, w in enumerate(POOL_WINDOWS):
            cols = slice(gi * gw, (gi + 1) * gw)
            per_t = []
            for t in range(ts):
                acc = None
                for i in range(w):
                    e = pb + t - i
                    term = st_ref[e, :, cols] if e < pb else u_ref[(e - pb) * bs:(e - pb + 1) * bs, cols]
                    acc = term if acc is None else acc + term
                cnt = float(min(PAST_LEN + t + 1, w))
                per_t.append(acc / cnt - u_ref[t * bs:(t + 1) * bs, cols])
            mixed.append(jnp.concatenate(per_t, axis=0))
        finish(mixed)


def _pool(h_main, st_pool_tm, w_grp, scale, l, col_pool, tb, bs, ts, n_prompt_steps, steps_per_seq):
    M = h_main.shape[0]
    _, pb, _, DP = st_pool_tm.shape
    G, gw = w_grp.shape[1], w_grp.shape[2]
    assert G == len(POOL_WINDOWS) and pb == max(POOL_WINDOWS) - 1 and pb <= POOL_HALO and col_pool % DP == 0
    body = functools.partial(_pool_body, tb=tb, bs=bs, ts=ts, pb=pb, gw=gw,
                             n_prompt_steps=n_prompt_steps, steps_per_seq=steps_per_seq)
    return pl.pallas_call(
        body,
        grid=(M // tb,),
        in_specs=[
            pl.BlockSpec((tb, DP), lambda s: (s, col_pool // DP)),
            pl.BlockSpec((None, pb, bs, DP), lambda s: (l, 0, 0, 0)),
            pl.BlockSpec((None, G, gw, gw), lambda s: (l, 0, 0, 0)),
            pl.BlockSpec((None, 1, DP), lambda s: (l, 0, 0)),
        ],
        out_specs=pl.BlockSpec((tb, DP), lambda s: (s, 0)),
        out_shape=jax.ShapeDtypeStruct((M, DP), BF16),
        scratch_shapes=[pltpu.VMEM((POOL_HALO + tb, DP), F32)],
        compiler_params=_params("arbitrary"),
        name="pool_mixer",
    )(h_main, st_pool_tm, w_grp, scale)


def _prep_body(x_ref, hs_ref, stc_ref, cw_ref, al_ref, dtb_ref, qkv_ref, gb_ref, ext_ref, halo_ref, *,
               tb, bs, ts, cw, nh, hk, chunk, n_prompt_steps, steps_per_seq):
    step = pl.program_id(0)
    sec = pl.program_id(1)
    half = V7X_LANES // 2

    def emit(conv_head):
        @pl.when(sec < 2)
        def _():
            qscale = jnp.where(sec == 0, hk ** -0.5, 1.0).astype(F32)
            for h in range(nh):
                y = conv_head(h)
                y = y * jax.nn.sigmoid(y)
                ss = jnp.sum(y * y, axis=-1, keepdims=True)
                qkv_ref[h] = y * (lax.rsqrt(ss + RMS_EPS) * qscale)

        @pl.when(sec == 2)
        def _():
            for h in range(nh):
                y = conv_head(h)
                qkv_ref[h] = y * jax.nn.sigmoid(y)

    def emit_gates(gcum, beta):
        for h in range(nh):
            gcol = jnp.broadcast_to(gcum[:, nh + h:nh + h + 1], (tb, half))
            bcol = jnp.broadcast_to(beta[:, h:h + 1], (tb, half))
            gb_ref[h] = jnp.concatenate([gcol, bcol], axis=1)

    def gate_values():
        hs = hs_ref[...]
        beta = jax.nn.sigmoid(hs)
        g = -jnp.exp(al_ref[...]) * _softplus(hs + dtb_ref[...])
        return g, beta

    @pl.when(step < n_prompt_steps)
    def _():
        blk = step % steps_per_seq

        @pl.when(blk == 0)
        def _():
            ext_ref[0:CONV_HALO, :] = jnp.zeros((CONV_HALO, ext_ref.shape[1]), F32)

        @pl.when(blk != 0)
        def _():
            ext_ref[0:CONV_HALO, :] = halo_ref[sec]

        ext_ref[CONV_HALO:CONV_HALO + tb, :] = x_ref[...]
        halo_ref[sec] = ext_ref[tb:tb + CONV_HALO, :]
        base = CONV_HALO - (cw - 1)

        def conv_head(h):
            cols = slice(h * hk, (h + 1) * hk)
            y = ext_ref[base:base + tb, cols] * cw_ref[0:1, cols]
            for i in range(1, cw):
                y = y + ext_ref[base + i:base + i + tb, cols] * cw_ref[i:i + 1, cols]
            return y

        emit(conv_head)

        @pl.when(sec == 0)
        def _():
            g, beta = gate_values()
            t_in = lax.broadcasted_iota(jnp.int32, g.shape, 0) % chunk
            d = 1
            while d < chunk:
                g = g + jnp.where(t_in >= d, pltpu.roll(g, d, 0), 0.0)
                d *= 2
            emit_gates(g, beta)

    @pl.when(step >= n_prompt_steps)
    def _():
        def ext_slab(j, cols):
            return stc_ref[j, :, cols] if j < cw - 1 else x_ref[(j - (cw - 1)) * bs:(j - (cw - 2)) * bs, cols]

        def conv_head(h):
            cols = slice(h * hk, (h + 1) * hk)
            per_t = []
            for t in range(ts):
                y = ext_slab(t, cols) * cw_ref[0:1, cols]
                for i in range(1, cw):
                    y = y + ext_slab(t + i, cols) * cw_ref[i:i + 1, cols]
                per_t.append(y)
            return jnp.concatenate(per_t, axis=0)

        emit(conv_head)

        @pl.when(sec == 0)
        def _():
            g, beta = gate_values()
            slabs = [g[0:bs]]
            for t in range(1, ts):
                slabs.append(slabs[-1] + g[t * bs:(t + 1) * bs])
            emit_gates(jnp.concatenate(slabs, axis=0), beta)


def _prep(h_main, st_conv_tm, conv_w, alog_row, dtb_row, l, col_qkv, col_small, tb, bs, ts, nh, hk,
          n_prompt_steps, steps_per_seq):
    M = h_main.shape[0]
    cw = conv_w.shape[1]
    sw = nh * hk
    assert cw - 1 <= CONV_HALO and col_qkv % sw == 0 and 2 * nh <= V7X_LANES and col_small % V7X_LANES == 0
    body = functools.partial(_prep_body, tb=tb, bs=bs, ts=ts, cw=cw, nh=nh, hk=hk, chunk=DELTA_CHUNK,
                             n_prompt_steps=n_prompt_steps, steps_per_seq=steps_per_seq)
    return pl.pallas_call(
        body,
        grid=(M // tb, 3),
        in_specs=[
            pl.BlockSpec((tb, sw), lambda s, c: (s, col_qkv // sw + c)),
            pl.BlockSpec((tb, V7X_LANES), lambda s, c: (s, col_small // V7X_LANES)),
            pl.BlockSpec((None, cw - 1, bs, sw), lambda s, c: (l, 0, 0, c)),
            pl.BlockSpec((None, cw, sw), lambda s, c: (l, 0, c)),
            pl.BlockSpec((None, 1, V7X_LANES), lambda s, c: (l, 0, 0)),
            pl.BlockSpec((None, 1, V7X_LANES), lambda s, c: (l, 0, 0)),
        ],
        out_specs=[
            pl.BlockSpec((nh, tb, hk), lambda s, c: (c, s, 0)),
            pl.BlockSpec((nh, tb, V7X_LANES), lambda s, c: (0, s, 0)),
        ],
        out_shape=[jax.ShapeDtypeStruct((3 * nh, M, hk), F32), jax.ShapeDtypeStruct((nh, M, V7X_LANES), F32)],
        scratch_shapes=[pltpu.VMEM((CONV_HALO + tb, sw), F32), pltpu.VMEM((3, CONV_HALO, sw), F32)],
        compiler_params=_params("arbitrary", "arbitrary"),
        name="delta_prep",
    )(h_main, h_main, st_conv_tm, conv_w, alog_row, dtb_row)


def _gated_norm(o, z, nw):
    o = o * lax.rsqrt(jnp.mean(o * o, axis=-1, keepdims=True) + RMS_EPS) * nw
    return o * (z * jax.nn.sigmoid(z))


def _delta_prompt_body(q_ref, k_ref, v_ref, gb_ref, z_ref, nw_ref, og_ref, so_ref, s_ref, *, hg, c, hk, hv):
    ci = pl.program_id(2)
    half = V7X_LANES // 2

    @pl.when(ci == 0)
    def _():
        s_ref[...] = jnp.zeros(s_ref.shape, F32)

    ti = lax.broadcasted_iota(jnp.int32, (c, c), 0)
    si = lax.broadcasted_iota(jnp.int32, (c, c), 1)
    incl = ti >= si
    strict = ti > si
    heads = range(hg)
    q = [q_ref[j] for j in heads]
    k = [k_ref[j] for j in heads]
    gbv = [gb_ref[j] for j in heads]
    g_col = [x[:, 0:1] for x in gbv]
    b_col = [x[:, half:half + 1] for x in gbv]
    decay = [jnp.where(incl, jnp.exp(jnp.where(incl, x[:, 0:c] - jnp.concatenate([x, x], axis=0).T[0:c, 0:c], 0.0)), 0.0)
             for x in gbv]
    kb = [k[j] * b_col[j] for j in heads]
    k16 = [x.astype(BF16) for x in k]
    a = [jnp.where(strict, _dot_t(kb[j].astype(BF16), k16[j]) * decay[j], 0.0) for j in heads]
    qk = [_dot_t(q[j].astype(BF16), k16[j]) * decay[j] for j in heads]
    n = [-x for x in a]
    p = [_dot(x.astype(BF16), x.astype(BF16)) for x in a]
    span = 4
    while span < c:
        both = [_dot(jnp.concatenate([n[j], p[j]], axis=0).astype(BF16), p[j].astype(BF16)) for j in heads]
        n = [n[j] + p[j] + both[j][0:c] for j in heads]
        p = [both[j][c:2 * c] for j in heads]
        span *= 2
    n = [n[j] + p[j] + _dot(n[j].astype(BF16), p[j].astype(BF16)) for j in heads]
    eg = [jnp.exp(x) for x in g_col]
    rhs = [jnp.concatenate([kb[j] * eg[j], v_ref[j] * b_col[j]], axis=1) for j in heads]
    tr = [rhs[j] + _dot(n[j].astype(BF16), rhs[j].astype(BF16)) for j in heads]
    g_last = [x[c - 1:c, :] for x in g_col]
    k_tail = [(k[j] * jnp.exp(g_last[j] - g_col[j])).astype(BF16) for j in heads]
    s0 = [s_ref[j] for j in heads]
    ps = [_dot(jnp.concatenate([tr[j][:, 0:hk], q[j] * eg[j]], axis=0).astype(BF16), s0[j].astype(BF16)) for j in heads]
    u = [(tr[j][:, hk:hk + hv] - ps[j][0:c]).astype(BF16) for j in heads]
    for j in heads:
        s_ref[j] = jnp.exp(g_last[j]) * s0[j] + _tdot(k_tail[j], u[j])
    o = [ps[j][c:2 * c] + _dot(qk[j].astype(BF16), u[j]) for j in heads]
    for j in heads:
        cols = slice(j * hv, (j + 1) * hv)
        og_ref[:, cols] = _gated_norm(o[j], z_ref[:, cols], nw_ref[...]).astype(og_ref.dtype)

    @pl.when(ci == pl.num_programs(2) - 1)
    def _():
        so_ref[...] = s_ref[...]


def _delta_prompt(qkv, gb, h_main, nw, l, col_z, nb, t, nh, hk, hv, hg):
    c = DELTA_CHUNK
    assert t % c == 0 and nh % hg == 0 and col_z % (hg * hv) == 0 and hk == V7X_LANES and c <= V7X_LANES // 2
    ncs = t // c
    ngr = nh // hg
    body = functools.partial(_delta_prompt_body, hg=hg, c=c, hk=hk, hv=hv)

    def head_spec(sec):
        return pl.BlockSpec((hg, c, hk), lambda b, g, ci: (sec * ngr + g, b * ncs + ci, 0))

    return pl.pallas_call(
        body,
        grid=(nb, ngr, ncs),
        in_specs=[
            head_spec(0), head_spec(1), head_spec(2),
            pl.BlockSpec((hg, c, V7X_LANES), lambda b, g, ci: (g, b * ncs + ci, 0)),
            pl.BlockSpec((c, hg * hv), lambda b, g, ci: (b * ncs + ci, col_z // (hg * hv) + g)),
            pl.BlockSpec((None, 1, hv), lambda b, g, ci: (l, 0, 0)),
        ],
        out_specs=[
            pl.BlockSpec((c, hg * hv), lambda b, g, ci: (b * ncs + ci, g)),
            pl.BlockSpec((None, hg, hk, hv), lambda b, g, ci: (b, g, 0, 0)),
        ],
        out_shape=[jax.ShapeDtypeStruct((nb * t, nh * hv), BF16), jax.ShapeDtypeStruct((nb, nh, hk, hv), F32)],
        scratch_shapes=[pltpu.VMEM((hg, hk, hv), F32)],
        compiler_params=_params("parallel", "parallel", "arbitrary"),
        name="delta_prompt",
    )(qkv, qkv, qkv, gb, h_main, nw)


def _delta_sample_body(q_ref, k_ref, v_ref, gb_ref, z_ref, nw_ref, s_ref, all_states_ref, og_ref, so_ref,
                       kq_scr, p_scr, u_scr, kt_scr, e_scr, *, bs, ts, bb, hk, hv):
    del all_states_ref
    b0 = pl.multiple_of(pl.program_id(1) * bb, bb)
    half = V7X_LANES // 2
    pad = V7X_SUBLANES - ts

    def rows(ref, t):
        return ref[pl.ds(t * bs + b0, bb), :]

    ks = [rows(k_ref, t) for t in range(ts)]
    qs = [rows(q_ref, t) for t in range(ts)]
    for t in range(ts):
        kq_scr[t * bb:(t + 1) * bb, :] = ks[t]
        kq_scr[(ts + t) * bb:(ts + t + 1) * bb, :] = qs[t]

    def state_products(g, carry):
        seqs = [g * SAMPLE_GROUP + i for i in range(SAMPLE_GROUP)]
        kq = [kq_scr[pl.ds(b, 2 * ts, stride=bb), :].astype(BF16) for b in seqs]
        s16 = [s_ref[b].astype(BF16) for b in seqs]
        prod = [_dot(kq[i], s16[i]) for i in range(SAMPLE_GROUP)]
        for i, b in enumerate(seqs):
            p_scr[pl.ds(b, 2 * ts, stride=bb), :] = prod[i]
        return carry

    lax.fori_loop(0, bb // SAMPLE_GROUP, state_products, 0)

    gs = [rows(gb_ref, t)[:, 0:1] for t in range(ts)]
    betas = [rows(gb_ref, t)[:, half:half + 1] for t in range(ts)]
    us = []
    for t in range(ts):
        u = betas[t] * (rows(v_ref, t) - jnp.exp(gs[t]) * p_scr[t * bb:(t + 1) * bb, :])
        for s in range(t):
            kk = jnp.sum(ks[t] * ks[s], axis=-1, keepdims=True)
            u = u - (betas[t] * kk * jnp.exp(gs[t] - gs[s])) * us[s]
        us.append(u)
    for t in range(ts):
        o = jnp.exp(gs[t]) * p_scr[(ts + t) * bb:(ts + t + 1) * bb, :]
        for s in range(t + 1):
            qk = jnp.sum(qs[t] * ks[s], axis=-1, keepdims=True)
            o = o + (qk * jnp.exp(gs[t] - gs[s])) * us[s]
        og_ref[pl.ds(t * bs + b0, bb), :] = _gated_norm(o, rows(z_ref, t), nw_ref[...]).astype(og_ref.dtype)
        u_scr[t * bb:(t + 1) * bb, :] = us[t]
        kt_scr[t * bb:(t + 1) * bb, :] = ks[t] * jnp.exp(gs[ts - 1] - gs[t])
    u_scr[ts * bb:(ts + pad) * bb, :] = jnp.zeros((pad * bb, hv), F32)
    kt_scr[ts * bb:(ts + pad) * bb, :] = jnp.zeros((pad * bb, hk), F32)
    e_scr[...] = jnp.broadcast_to(jnp.exp(gs[ts - 1]), (bb, hv))

    def state_update(g, carry):
        seqs = [g * SAMPLE_GROUP + i for i in range(SAMPLE_GROUP)]
        ub = [u_scr[pl.ds(b, V7X_SUBLANES, stride=bb), :].astype(BF16) for b in seqs]
        kb = [kt_scr[pl.ds(b, V7X_SUBLANES, stride=bb), :].astype(BF16) for b in seqs]
        upd = [_tdot(kb[i], ub[i]) for i in range(SAMPLE_GROUP)]
        for i, b in enumerate(seqs):
            so_ref[b] = e_scr[pl.ds(b, 1), :] * s_ref[b] + upd[i]
        return carry

    lax.fori_loop(0, bb // SAMPLE_GROUP, state_update, 0)


def _delta_sample(qkv, gb, h_main, nw, state, new_states, l, col_z, row0, bs, ts, nh, hk, hv):
    tbs = bs * ts
    bb = _divisor(bs, 64, V7X_BF16_ROWS)
    assert row0 % tbs == 0 and col_z % hv == 0 and 2 * ts <= V7X_SUBLANES and bb % SAMPLE_GROUP == 0
    body = functools.partial(_delta_sample_body, bs=bs, ts=ts, bb=bb, hk=hk, hv=hv)

    def head_spec(sec):
        return pl.BlockSpec((None, tbs, hk), lambda h, g: (sec * nh + h, row0 // tbs, 0))

    return pl.pallas_call(
        body,
        grid=(nh, bs // bb),
        in_specs=[
            head_spec(0), head_spec(1), head_spec(2),
            pl.BlockSpec((None, tbs, V7X_LANES), lambda h, g: (h, row0 // tbs, 0)),
            pl.BlockSpec((tbs, hv), lambda h, g: (row0 // tbs, col_z // hv + h)),
            pl.BlockSpec((None, 1, hv), lambda h, g: (l, 0, 0)),
            pl.BlockSpec((None, bb, None, hk, hv), lambda h, g: (l, g, h, 0, 0)),
            pl.BlockSpec(memory_space=pl.ANY),
        ],
        out_specs=[
            pl.BlockSpec((tbs, hv), lambda h, g: (0, h)),
            pl.BlockSpec((None, bb, None, hk, hv), lambda h, g: (l, g, h, 0, 0)),
        ],
        out_shape=[jax.ShapeDtypeStruct((tbs, nh * hv), BF16), jax.ShapeDtypeStruct(new_states.shape, F32)],
        input_output_aliases={7: 1},
        scratch_shapes=[
            pltpu.VMEM((2 * ts * bb, hk), F32),
            pltpu.VMEM((2 * ts * bb, hv), F32),
            pltpu.VMEM((V7X_SUBLANES * bb, hv), F32),
            pltpu.VMEM((V7X_SUBLANES * bb, hk), F32),
            pltpu.VMEM((bb, hv), F32),
        ],
        compiler_params=_params("parallel", "arbitrary"),
        name="delta_sample",
    )(qkv, qkv, qkv, gb, h_main, nw, state, new_states)


def kernel(x_prompt, x_sample, state_pool, state_conv, state_delta, w_in, w_pool_grp, pool_scale, w_pool_up,
           conv_w, a_log, dt_bias, o_norm_w, w_delta_up, w_out, ln1_g, ln1_b, w_gate, w_up, w_down, ln2_g, ln2_b):
    nb, t, d = x_prompt.shape
    bs, ts, _ = x_sample.shape
    depth = w_in.shape[0]
    pb, dp = state_pool.shape[2], state_pool.shape[3]
    cwm1 = state_conv.shape[2]
    assert t >= pb and t >= cwm1
    dqkv = state_conv.shape[3]
    nh, hk, hv = state_delta.shape[2], state_delta.shape[3], state_delta.shape[4]
    dv = nh * hv
    dff = w_gate.shape[-1]
    assert dqkv == 2 * nh * hk + dv and hk == hv
    mp, ms = nb * t, bs * ts
    m = mp + ms
    tb = ms
    assert t % tb == 0 and tb % DELTA_CHUNK == 0
    steps_per_seq = t // tb
    n_prompt_steps = mp // tb
    alpha = float((2 * depth) ** 0.25)

    off_qkv = dp
    off_z = off_qkv + dqkv
    off_beta = off_z + dv
    off_ga = off_beta + 2 * nh
    off_gb = off_ga + d
    assert w_in.shape[-1] == off_gb + d
    col_qkv, col_z, col_pool = 0, dqkv, dqkv + dv
    col_ga = col_pool + dp
    col_gb = col_ga + d
    col_small = col_gb + d
    segments = ((off_qkv, col_qkv, dqkv), (off_z, col_z, dv), (0, col_pool, dp), (off_ga, col_ga, 2 * d),
                (off_beta, col_small, INPROJ_BLOCK_N))
    assert off_beta + INPROJ_BLOCK_N <= w_in.shape[-1]
    w_in_t = jnp.swapaxes(w_in, 1, 2)
    w_grp16 = w_pool_grp.astype(BF16)
    w_pu16 = w_pool_up.astype(BF16)
    w_du16 = w_delta_up.astype(BF16)

    lane_pad = ((0, 0), (nh, V7X_LANES - 2 * nh))
    alog_row = jnp.pad(a_log, lane_pad)[:, None, :]
    dtb_row = jnp.pad(dt_bias, lane_pad)[:, None, :]
    scale_row = pool_scale[:, None, :]
    nw_row = o_norm_w[:, None, :]
    ln1_g, ln1_b, ln2_g, ln2_b = (a[:, None, :] for a in (ln1_g, ln1_b, ln2_g, ln2_b))
    st_pool_tm = jnp.swapaxes(state_pool, 1, 2)
    st_conv_tm = jnp.swapaxes(state_conv, 1, 2)

    x = jnp.concatenate([x_prompt.reshape(mp, d), jnp.swapaxes(x_sample, 0, 1).reshape(ms, d)], axis=0)
    x16 = x.astype(BF16)

    pool_p, conv_p, delta_p, pool_s, conv_s = [], [], [], [], []
    new_delta_s = jnp.zeros(state_delta.shape, F32)
    for l in range(depth):
        h_main = _inproj(x16, w_in_t, segments, l)
        yp = _pool(h_main, st_pool_tm, w_grp16, scale_row, l, col_pool, tb, bs, ts, n_prompt_steps, steps_per_seq)
        qkv, gb = _prep(h_main, st_conv_tm, conv_w, alog_row, dtb_row, l, col_qkv, col_small, tb, bs, ts, nh, hk,
                        n_prompt_steps, steps_per_seq)
        og_p, s_p = _delta_prompt(qkv, gb, h_main, nw_row, l, col_z, nb, t, nh, hk, hv, hg=min(nh, 16))
        og_s, new_delta_s = _delta_sample(qkv, gb, h_main, nw_row, state_delta, new_delta_s, l, col_z, mp, bs, ts,
                                          nh, hk, hv)
        merged = _merge(yp, og_p, og_s, w_pu16, w_du16, h_main, l, col_ga, col_gb, tb)
        r1 = _outproj(merged, w_out, x, l, alpha)
        x1, x1_16 = _layer_norm(r1, ln1_g, ln1_b, l)
        act, w_down16 = _glu(x1_16, w_gate, w_up, w_down, l)
        r2 = _down(act, w_down16, x1, alpha)
        if l + 1 < depth:
            x, x16 = _layer_norm(r2, ln2_g, ln2_b, l)
        else:
            y_p, y_s = _layer_norm_split(r2, ln2_g, ln2_b, l, mp)

        def tail_rows(col, width, keep):
            return jnp.stack([h_main[(b + 1) * t - keep:(b + 1) * t, col:col + width] for b in range(nb)])

        def sample_rows(col, width):
            return jnp.swapaxes(h_main[mp:, col:col + width].reshape(ts, bs, width), 0, 1)

        pool_p.append(tail_rows(col_pool, dp, pb))
        conv_p.append(tail_rows(col_qkv, dqkv, cwm1))
        pool_s.append(jnp.concatenate([state_pool[l], sample_rows(col_pool, dp)], axis=1)[:, -pb:])
        conv_s.append(jnp.concatenate([state_conv[l], sample_rows(col_qkv, dqkv)], axis=1)[:, -cwm1:])
        delta_p.append(s_p)

    y_prompt = y_p.reshape(nb, t, d)
    y_sample = jnp.swapaxes(y_s.reshape(ts, bs, d), 0, 1)
    return (y_prompt, y_sample, jnp.stack(pool_p), jnp.stack(conv_p), jnp.stack(delta_p),
            jnp.stack(pool_s), jnp.stack(conv_s), new_delta_s)
```

```python
import functools
import math

import jax
import jax.numpy as jnp
from jax import lax
from jax.experimental import pallas as pl
from jax.experimental.pallas import tpu as pltpu

F32 = jnp.float32
BF16 = jnp.bfloat16

POOL_WINDOWS = (2, 4, 8, 16)
DELTA_CHUNK = 64
PAST_LEN = 16384
LN_EPS = 1e-5
RMS_EPS = 1e-6

V7X_LANES = 128
V7X_SUBLANES = 8
V7X_BF16_ROWS = 16
V7X_VMEM_LIMIT_BYTES = 56 * 1024 * 1024
POOL_HALO = 16
CONV_HALO = V7X_SUBLANES
GLU_BLOCK_N = 256
INPROJ_BLOCK_N = 512
SAMPLE_GROUP = 8


def _params(*semantics):
    return pltpu.CompilerParams(dimension_semantics=semantics, vmem_limit_bytes=V7X_VMEM_LIMIT_BYTES)


def _divisor(n, target, mult):
    best = None
    for d in range(mult, min(n, target) + 1, mult):
        if n % d == 0:
            best = d
    assert best is not None, (n, target, mult)
    return best


def _dot(a, b):
    return jnp.dot(a, b, preferred_element_type=F32)


def _dot_t(a, b):
    return lax.dot_general(a, b, (((1,), (1,)), ((), ())), preferred_element_type=F32)


def _tdot(a, b):
    return lax.dot_general(a, b, (((0,), (0,)), ((), ())), preferred_element_type=F32)


def _softplus(x):
    return jnp.maximum(x, 0.0) + jnp.log1p(jnp.exp(-jnp.abs(x)))


def _inproj_body(x_ref, w_ref, o_ref):
    x = x_ref[...]
    o_ref[...] = _dot(x, w_ref[0].astype(x.dtype).T)


def _inproj(x, w_in_t, segments, l):
    M, K = x.shape
    n_out = sum(width for _, _, width in segments)
    bm = _divisor(M, 2176, V7X_BF16_ROWS)
    bn = INPROJ_BLOCK_N
    assert all(src % V7X_SUBLANES == 0 and dst % bn == 0 and width % bn == 0 for src, dst, width in segments)

    def src_row(j):
        col = j * bn
        row = 0
        for src, dst, width in segments:
            row = jnp.where((col >= dst) & (col < dst + width), src + col - dst, row)
        return pl.multiple_of(row, V7X_SUBLANES)

    return pl.pallas_call(
        _inproj_body,
        grid=(M // bm, n_out // bn),
        in_specs=[
            pl.BlockSpec((bm, K), lambda i, j: (i, 0), pipeline_mode=pl.Buffered(1)),
            pl.BlockSpec((pl.Element(1), pl.Element(bn), pl.Element(K)), lambda i, j: (l, src_row(j), 0)),
        ],
        out_specs=pl.BlockSpec((bm, bn), lambda i, j: (i, j)),
        out_shape=jax.ShapeDtypeStruct((M, n_out), F32),
        compiler_params=_params("parallel", "arbitrary"),
        name="inproj",
    )(x, w_in_t)


def _merge_body(yp_ref, ogp_ref, ogs_ref, wpu_ref, wdu_ref, ga_ref, gb_ref, o_ref, *, n_prompt_blocks):
    ga = 0.5 * jnp.tanh(0.5 * ga_ref[...]) + 0.5
    gb = 0.5 * jnp.tanh(0.5 * gb_ref[...]) + 0.5
    og = jnp.where(pl.program_id(1) < n_prompt_blocks, ogp_ref[...], ogs_ref[...])
    ya = _dot(yp_ref[...], wpu_ref[...])
    yb = _dot(og, wdu_ref[...])
    o_ref[...] = (ga * ya + gb * yb).astype(o_ref.dtype)


def _merge(yp, og_p, og_s, w_pool_up, w_delta_up, h_main, l, col_ga, col_gb, tb):
    M, DP = yp.shape
    DV = og_p.shape[1]
    D = w_pool_up.shape[-1]
    bn = _divisor(math.gcd(D, col_ga, col_gb), 2048, V7X_LANES)
    npb = og_p.shape[0] // tb
    assert og_s.shape[0] == tb and col_ga % bn == 0 and col_gb % bn == 0
    return pl.pallas_call(
        functools.partial(_merge_body, n_prompt_blocks=npb),
        grid=(D // bn, M // tb),
        in_specs=[
            pl.BlockSpec((tb, DP), lambda j, i: (i, 0)),
            pl.BlockSpec((tb, DV), lambda j, i: (jnp.minimum(i, npb - 1), 0)),
            pl.BlockSpec((tb, DV), lambda j, i: (0, 0), pipeline_mode=pl.Buffered(1)),
            pl.BlockSpec((None, DP, bn), lambda j, i: (l, 0, j), pipeline_mode=pl.Buffered(1)),
            pl.BlockSpec((None, DV, bn), lambda j, i: (l, 0, j), pipeline_mode=pl.Buffered(1)),
            pl.BlockSpec((tb, bn), lambda j, i: (i, col_ga // bn + j)),
            pl.BlockSpec((tb, bn), lambda j, i: (i, col_gb // bn + j)),
        ],
        out_specs=pl.BlockSpec((tb, bn), lambda j, i: (i, j)),
        out_shape=jax.ShapeDtypeStruct((M, D), BF16),
        compiler_params=_params("parallel", "parallel"),
        name="merge",
    )(yp, og_p, og_s, w_pool_up, w_delta_up, h_main, h_main)


def _outproj_body(a_ref, w_ref, x_ref, o_ref, *, alpha):
    a = a_ref[...]
    o_ref[...] = alpha * x_ref[...] + _dot(a, w_ref[...].astype(a.dtype))


def _outproj(a, w, x, l, alpha):
    M, K = a.shape
    N = w.shape[-1]
    bm = _divisor(M, 1088, V7X_BF16_ROWS)
    bn = _divisor(N, 512, V7X_LANES)
    return pl.pallas_call(
        functools.partial(_outproj_body, alpha=alpha),
        grid=(M // bm, N // bn),
        in_specs=[
            pl.BlockSpec((bm, K), lambda i, j: (i, 0)),
            pl.BlockSpec((None, K, bn), lambda i, j: (l, 0, j)),
            pl.BlockSpec((bm, bn), lambda i, j: (i, j)),
        ],
        out_specs=pl.BlockSpec((bm, bn), lambda i, j: (i, j)),
        out_shape=jax.ShapeDtypeStruct((M, N), F32),
        compiler_params=_params("parallel", "parallel"),
        name="outproj",
    )(a, w, x)


def _glu_body(x_ref, wg_ref, wu_ref, wd_ref, o_ref, wd16_ref, *, n_cast_steps):
    x = x_ref[...]
    g = _dot(x, wg_ref[...].astype(x.dtype))
    u = _dot(x, wu_ref[...].astype(x.dtype))
    o_ref[...] = (g * jax.nn.sigmoid(g) * u).astype(o_ref.dtype)

    @pl.when(pl.program_id(0) * pl.num_programs(1) + pl.program_id(1) < n_cast_steps)
    def _():
        wd16_ref[...] = wd_ref[...].astype(wd16_ref.dtype)


def _glu(x, w_gate, w_up, w_down, l):
    M, K = x.shape
    N = w_gate.shape[-1]
    F, D = w_down.shape[1], w_down.shape[2]
    bm = _divisor(M, 2176, V7X_BF16_ROWS)
    bn = _divisor(N, GLU_BLOCK_N, V7X_LANES)
    nj = N // bn
    steps = (M // bm) * nj
    slab = min(s for s in range(V7X_BF16_ROWS, F + 1, V7X_BF16_ROWS) if F % s == 0 and s >= 128 and F // s <= steps)
    n_cast_steps = F // slab

    def slab_index(i, j):
        return jnp.minimum(i * nj + j, n_cast_steps - 1)

    return pl.pallas_call(
        functools.partial(_glu_body, n_cast_steps=n_cast_steps),
        grid=(M // bm, nj),
        in_specs=[
            pl.BlockSpec((bm, K), lambda i, j: (i, 0), pipeline_mode=pl.Buffered(1)),
            pl.BlockSpec((None, K, bn), lambda i, j: (l, 0, j)),
            pl.BlockSpec((None, K, bn), lambda i, j: (l, 0, j)),
            pl.BlockSpec((None, slab, D), lambda i, j: (l, slab_index(i, j), 0)),
        ],
        out_specs=[
            pl.BlockSpec((bm, bn), lambda i, j: (i, j)),
            pl.BlockSpec((slab, D), lambda i, j: (slab_index(i, j), 0)),
        ],
        out_shape=[jax.ShapeDtypeStruct((M, N), BF16), jax.ShapeDtypeStruct((F, D), BF16)],
        compiler_params=_params("arbitrary", "arbitrary"),
        name="glu",
    )(x, w_gate, w_up, w_down)


def _down_body(a_ref, w_ref, x_ref, o_ref, *, alpha):
    @pl.when(pl.program_id(2) == 0)
    def _():
        o_ref[...] = alpha * x_ref[...]

    o_ref[...] += _dot(a_ref[...], w_ref[...])


def _down(a, w, x, alpha):
    M, K = a.shape
    N = w.shape[-1]
    bm = _divisor(M, 1088, V7X_BF16_ROWS)
    bn = _divisor(N, 512, V7X_LANES)
    tk = _divisor(K, 5504, V7X_LANES)
    return pl.pallas_call(
        functools.partial(_down_body, alpha=alpha),
        grid=(M // bm, N // bn, K // tk),
        in_specs=[
            pl.BlockSpec((bm, tk), lambda i, j, k: (i, k)),
            pl.BlockSpec((tk, bn), lambda i, j, k: (k, j)),
            pl.BlockSpec((bm, bn), lambda i, j, k: (i, j)),
        ],
        out_specs=pl.BlockSpec((bm, bn), lambda i, j, k: (i, j)),
        out_shape=jax.ShapeDtypeStruct((M, N), F32),
        compiler_params=_params("parallel", "parallel", "arbitrary"),
        name="down",
    )(a, w, x)


def _ln_body(r_ref, g_ref, b_ref, xf_ref, xb_ref):
    r = r_ref[...]
    mu = jnp.mean(r, axis=-1, keepdims=True)
    xc = r - mu
    var = jnp.mean(xc * xc, axis=-1, keepdims=True)
    y = xc * lax.rsqrt(var + LN_EPS) * g_ref[...] + b_ref[...]
    xf_ref[...] = y
    xb_ref[...] = y.astype(xb_ref.dtype)


def _ln_split_body(r_ref, g_ref, b_ref, yp_ref, ys_ref, *, n_prompt_blocks):
    r = r_ref[...]
    mu = jnp.mean(r, axis=-1, keepdims=True)
    xc = r - mu
    var = jnp.mean(xc * xc, axis=-1, keepdims=True)
    y = xc * lax.rsqrt(var + LN_EPS) * g_ref[...] + b_ref[...]

    @pl.when(pl.program_id(0) < n_prompt_blocks)
    def _():
        yp_ref[...] = y

    @pl.when(pl.program_id(0) >= n_prompt_blocks)
    def _():
        ys_ref[...] = y


def _layer_norm_split(r, g, b, l, mp):
    M, D = r.shape
    ms = M - mp
    bm = _divisor(math.gcd(mp, ms), 272, V7X_SUBLANES)
    npb = mp // bm
    return pl.pallas_call(
        functools.partial(_ln_split_body, n_prompt_blocks=npb),
        grid=(M // bm,),
        in_specs=[
            pl.BlockSpec((bm, D), lambda i: (i, 0)),
            pl.BlockSpec((None, 1, D), lambda i: (l, 0, 0)),
            pl.BlockSpec((None, 1, D), lambda i: (l, 0, 0)),
        ],
        out_specs=[
            pl.BlockSpec((bm, D), lambda i: (jnp.minimum(i, npb - 1), 0)),
            pl.BlockSpec((bm, D), lambda i: (jnp.maximum(i - npb, 0), 0)),
        ],
        out_shape=[jax.ShapeDtypeStruct((mp, D), F32), jax.ShapeDtypeStruct((ms, D), F32)],
        compiler_params=_params("arbitrary"),
        name="layer_norm_split",
    )(r, g, b)


def _layer_norm(r, g, b, l):
    M, D = r.shape
    bm = _divisor(M, 272, V7X_BF16_ROWS)
    return pl.pallas_call(
        _ln_body,
        grid=(M // bm,),
        in_specs=[
            pl.BlockSpec((bm, D), lambda i: (i, 0)),
            pl.BlockSpec((None, 1, D), lambda i: (l, 0, 0)),
            pl.BlockSpec((None, 1, D), lambda i: (l, 0, 0)),
        ],
        out_specs=[pl.BlockSpec((bm, D), lambda i: (i, 0)), pl.BlockSpec((bm, D), lambda i: (i, 0))],
        out_shape=[jax.ShapeDtypeStruct((M, D), F32), jax.ShapeDtypeStruct((M, D), BF16)],
        compiler_params=_params("parallel"),
        name="layer_norm",
    )(r, g, b)


def _pool_body(u_ref, st_ref, wg_ref, sc_ref, o_ref, ext_ref, *, tb, bs, ts, pb, gw, n_prompt_steps, steps_per_seq):
    step = pl.program_id(0)

    def finish(mixed):
        for gi, m in enumerate(mixed):
            cols = slice(gi * gw, (gi + 1) * gw)
            y = _dot(m.astype(BF16), wg_ref[gi])
            o_ref[:, cols] = (y * sc_ref[:, cols]).astype(o_ref.dtype)

    @pl.when(step < n_prompt_steps)
    def _():
        blk = step % steps_per_seq

        @pl.when(blk == 0)
        def _():
            ext_ref[0:POOL_HALO, :] = jnp.zeros((POOL_HALO, ext_ref.shape[1]), F32)

        ext_ref[POOL_HALO:POOL_HALO + tb, :] = u_ref[...]
        pos = blk * tb + lax.broadcasted_iota(jnp.int32, (tb, 1), 0)
        mixed = []
        for gi, w in enumerate(POOL_WINDOWS):
            cols = slice(gi * gw, (gi + 1) * gw)
            cur = ext_ref[POOL_HALO:POOL_HALO + tb, cols]
            win = cur
            for i in range(1, w):
                win = win + ext_ref[POOL_HALO - i:POOL_HALO - i + tb, cols]
            cnt = jnp.minimum(pos + 1, w).astype(F32)
            mixed.append(win / cnt - cur)
        finish(mixed)
        ext_ref[0:POOL_HALO, :] = ext_ref[tb:tb + POOL_HALO, :]

    @pl.when(step >= n_prompt_steps)
    def _():
        mixed = []
        for gi, w in enumerate(POOL_WINDOWS):
            cols = slice(gi * gw, (gi + 1) * gw)
            per_t = []
            for t in range(ts):
                acc = None
                for i in range(w):
                    e = pb + t - i
                    term = st_ref[e, :, cols] if e < pb else u_ref[(e - pb) * bs:(e - pb + 1) * bs, cols]
                    acc = term if acc is None else acc + term
                cnt = float(min(PAST_LEN + t + 1, w))
                per_t.append(acc / cnt - u_ref[t * bs:(t + 1) * bs, cols])
            mixed.append(jnp.concatenate(per_t, axis=0))
        finish(mixed)


def _pool(h_main, st_pool_tm, w_grp, scale, l, col_pool, tb, bs, ts, n_prompt_steps, steps_per_seq):
    M = h_main.shape[0]
    _, pb, _, DP = st_pool_tm.shape
    G, gw = w_grp.shape[1], w_grp.shape[2]
    assert G == len(POOL_WINDOWS) and pb == max(POOL_WINDOWS) - 1 and pb <= POOL_HALO and col_pool % DP == 0
    body = functools.partial(_pool_body, tb=tb, bs=bs, ts=ts, pb=pb, gw=gw,
                             n_prompt_steps=n_prompt_steps, steps_per_seq=steps_per_seq)
    return pl.pallas_call(
        body,
        grid=(M // tb,),
        in_specs=[
            pl.BlockSpec((tb, DP), lambda s: (s, col_pool // DP)),
            pl.BlockSpec((None, pb, bs, DP), lambda s: (l, 0, 0, 0)),
            pl.BlockSpec((None, G, gw, gw), lambda s: (l, 0, 0, 0)),
            pl.BlockSpec((None, 1, DP), lambda s: (l, 0, 0)),
        ],
        out_specs=pl.BlockSpec((tb, DP), lambda s: (s, 0)),
        out_shape=jax.ShapeDtypeStruct((M, DP), BF16),
        scratch_shapes=[pltpu.VMEM((POOL_HALO + tb, DP), F32)],
        compiler_params=_params("arbitrary"),
        name="pool_mixer",
    )(h_main, st_pool_tm, w_grp, scale)


def _prep_body(x_ref, hs_ref, stc_ref, cw_ref, al_ref, dtb_ref, qkv_ref, gb_ref, ext_ref, halo_ref, *,
               tb, bs, ts, cw, nh, hk, chunk, n_prompt_steps, steps_per_seq):
    step = pl.program_id(0)
    sec = pl.program_id(1)
    half = V7X_LANES // 2

    def emit(conv_head):
        @pl.when(sec < 2)
        def _():
            qscale = jnp.where(sec == 0, hk ** -0.5, 1.0).astype(F32)
            for h in range(nh):
                y = conv_head(h)
                y = y * jax.nn.sigmoid(y)
                ss = jnp.sum(y * y, axis=-1, keepdims=True)
                qkv_ref[h] = y * (lax.rsqrt(ss + RMS_EPS) * qscale)

        @pl.when(sec == 2)
        def _():
            for h in range(nh):
                y = conv_head(h)
                qkv_ref[h] = y * jax.nn.sigmoid(y)

    def emit_gates(gcum, beta):
        for h in range(nh):
            gcol = jnp.broadcast_to(gcum[:, nh + h:nh + h + 1], (tb, half))
            bcol = jnp.broadcast_to(beta[:, h:h + 1], (tb, half))
            gb_ref[h] = jnp.concatenate([gcol, bcol], axis=1)

    def gate_values():
        hs = hs_ref[...]
        beta = jax.nn.sigmoid(hs)
        g = -jnp.exp(al_ref[...]) * _softplus(hs + dtb_ref[...])
        return g, beta

    @pl.when(step < n_prompt_steps)
    def _():
        blk = step % steps_per_seq

        @pl.when(blk == 0)
        def _():
            ext_ref[0:CONV_HALO, :] = jnp.zeros((CONV_HALO, ext_ref.shape[1]), F32)

        @pl.when(blk != 0)
        def _():
            ext_ref[0:CONV_HALO, :] = halo_ref[sec]

        ext_ref[CONV_HALO:CONV_HALO + tb, :] = x_ref[...]
        halo_ref[sec] = ext_ref[tb:tb + CONV_HALO, :]
        base = CONV_HALO - (cw - 1)

        def conv_head(h):
            cols = slice(h * hk, (h + 1) * hk)
            y = ext_ref[base:base + tb, cols] * cw_ref[0:1, cols]
            for i in range(1, cw):
                y = y + ext_ref[base + i:base + i + tb, cols] * cw_ref[i:i + 1, cols]
            return y

        emit(conv_head)

        @pl.when(sec == 0)
        def _():
            g, beta = gate_values()
            t_in = lax.broadcasted_iota(jnp.int32, g.shape, 0) % chunk
            d = 1
            while d < chunk:
                g = g + jnp.where(t_in >= d, pltpu.roll(g, d, 0), 0.0)
                d *= 2
            emit_gates(g, beta)

    @pl.when(step >= n_prompt_steps)
    def _():
        def ext_slab(j, cols):
            return stc_ref[j, :, cols] if j < cw - 1 else x_ref[(j - (cw - 1)) * bs:(j - (cw - 2)) * bs, cols]

        def conv_head(h):
            cols = slice(h * hk, (h + 1) * hk)
            per_t = []
            for t in range(ts):
                y = ext_slab(t, cols) * cw_ref[0:1, cols]
                for i in range(1, cw):
                    y = y + ext_slab(t + i, cols) * cw_ref[i:i + 1, cols]
                per_t.append(y)
            return jnp.concatenate(per_t, axis=0)

        emit(conv_head)

        @pl.when(sec == 0)
        def _():
            g, beta = gate_values()
            slabs = [g[0:bs]]
            for t in range(1, ts):
                slabs.append(slabs[-1] + g[t * bs:(t + 1) * bs])
            emit_gates(jnp.concatenate(slabs, axis=0), beta)


def _prep(h_main, st_conv_tm, conv_w, alog_row, dtb_row, l, col_qkv, col_small, tb, bs, ts, nh, hk,
          n_prompt_steps, steps_per_seq):
    M = h_main.shape[0]
    cw = conv_w.shape[1]
    sw = nh * hk
    assert cw - 1 <= CONV_HALO and col_qkv % sw == 0 and 2 * nh <= V7X_LANES and col_small % V7X_LANES == 0
    body = functools.partial(_prep_body, tb=tb, bs=bs, ts=ts, cw=cw, nh=nh, hk=hk, chunk=DELTA_CHUNK,
                             n_prompt_steps=n_prompt_steps, steps_per_seq=steps_per_seq)
    return pl.pallas_call(
        body,
        grid=(M // tb, 3),
        in_specs=[
            pl.BlockSpec((tb, sw), lambda s, c: (s, col_qkv // sw + c)),
            pl.BlockSpec((tb, V7X_LANES), lambda s, c: (s, col_small // V7X_LANES)),
            pl.BlockSpec((None, cw - 1, bs, sw), lambda s, c: (l, 0, 0, c)),
            pl.BlockSpec((None, cw, sw), lambda s, c: (l, 0, c)),
            pl.BlockSpec((None, 1, V7X_LANES), lambda s, c: (l, 0, 0)),
            pl.BlockSpec((None, 1, V7X_LANES), lambda s, c: (l, 0, 0)),
        ],
        out_specs=[
            pl.BlockSpec((nh, tb, hk), lambda s, c: (c, s, 0)),
            pl.BlockSpec((nh, tb, V7X_LANES), lambda s, c: (0, s, 0)),
        ],
        out_shape=[jax.ShapeDtypeStruct((3 * nh, M, hk), F32), jax.ShapeDtypeStruct((nh, M, V7X_LANES), F32)],
        scratch_shapes=[pltpu.VMEM((CONV_HALO + tb, sw), F32), pltpu.VMEM((3, CONV_HALO, sw), F32)],
        compiler_params=_params("arbitrary", "arbitrary"),
        name="delta_prep",
    )(h_main, h_main, st_conv_tm, conv_w, alog_row, dtb_row)


def _gated_norm(o, z, nw):
    o = o * lax.rsqrt(jnp.mean(o * o, axis=-1, keepdims=True) + RMS_EPS) * nw
    return o * (z * jax.nn.sigmoid(z))


def _delta_prompt_body(q_ref, k_ref, v_ref, gb_ref, z_ref, nw_ref, og_ref, so_ref, s_ref, *, hg, c, hk, hv):
    ci = pl.program_id(2)
    half = V7X_LANES // 2

    @pl.when(ci == 0)
    def _():
        s_ref[...] = jnp.zeros(s_ref.shape, F32)

    ti = lax.broadcasted_iota(jnp.int32, (c, c), 0)
    si = lax.broadcasted_iota(jnp.int32, (c, c), 1)
    incl = ti >= si
    strict = ti > si
    heads = range(hg)
    q = [q_ref[j] for j in heads]
    k = [k_ref[j] for j in heads]
    gbv = [gb_ref[j] for j in heads]
    g_col = [x[:, 0:1] for x in gbv]
    b_col = [x[:, half:half + 1] for x in gbv]
    decay = [jnp.where(incl, jnp.exp(jnp.where(incl, x[:, 0:c] - jnp.concatenate([x, x], axis=0).T[0:c, 0:c], 0.0)), 0.0)
             for x in gbv]
    kb = [k[j] * b_col[j] for j in heads]
    k16 = [x.astype(BF16) for x in k]
    a = [jnp.where(strict, _dot_t(kb[j].astype(BF16), k16[j]) * decay[j], 0.0) for j in heads]
    qk = [_dot_t(q[j].astype(BF16), k16[j]) * decay[j] for j in heads]
    n = [-x for x in a]
    p = [_dot(x.astype(BF16), x.astype(BF16)) for x in a]
    span = 4
    while span < c:
        both = [_dot(jnp.concatenate([n[j], p[j]], axis=0).astype(BF16), p[j].astype(BF16)) for j in heads]
        n = [n[j] + p[j] + both[j][0:c] for j in heads]
        p = [both[j][c:2 * c] for j in heads]
        span *= 2
    n = [n[j] + p[j] + _dot(n[j].astype(BF16), p[j].astype(BF16)) for j in heads]
    eg = [jnp.exp(x) for x in g_col]
    rhs = [jnp.concatenate([kb[j] * eg[j], v_ref[j] * b_col[j]], axis=1) for j in heads]
    tr = [rhs[j] + _dot(n[j].astype(BF16), rhs[j].astype(BF16)) for j in heads]
    g_last = [x[c - 1:c, :] for x in g_col]
    k_tail = [(k[j] * jnp.exp(g_last[j] - g_col[j])).astype(BF16) for j in heads]
    s0 = [s_ref[j] for j in heads]
    ps = [_dot(jnp.concatenate([tr[j][:, 0:hk], q[j] * eg[j]], axis=0).astype(BF16), s0[j].astype(BF16)) for j in heads]
    u = [(tr[j][:, hk:hk + hv] - ps[j][0:c]).astype(BF16) for j in heads]
    for j in heads:
        s_ref[j] = jnp.exp(g_last[j]) * s0[j] + _tdot(k_tail[j], u[j])
    o = [ps[j][c:2 * c] + _dot(qk[j].astype(BF16), u[j]) for j in heads]
    for j in heads:
        cols = slice(j * hv, (j + 1) * hv)
        og_ref[:, cols] = _gated_norm(o[j], z_ref[:, cols], nw_ref[...]).astype(og_ref.dtype)

    @pl.when(ci == pl.num_programs(2) - 1)
    def _():
        so_ref[...] = s_ref[...]


def _delta_prompt(qkv, gb, h_main, nw, l, col_z, nb, t, nh, hk, hv, hg):
    c = DELTA_CHUNK
    assert t % c == 0 and nh % hg == 0 and col_z % (hg * hv) == 0 and hk == V7X_LANES and c <= V7X_LANES // 2
    ncs = t // c
    ngr = nh // hg
    body = functools.partial(_delta_prompt_body, hg=hg, c=c, hk=hk, hv=hv)

    def head_spec(sec):
        return pl.BlockSpec((hg, c, hk), lambda b, g, ci: (sec * ngr + g, b * ncs + ci, 0))

    return pl.pallas_call(
        body,
        grid=(nb, ngr, ncs),
        in_specs=[
            head_spec(0), head_spec(1), head_spec(2),
            pl.BlockSpec((hg, c, V7X_LANES), lambda b, g, ci: (g, b * ncs + ci, 0)),
            pl.BlockSpec((c, hg * hv), lambda b, g, ci: (b * ncs + ci, col_z // (hg * hv) + g)),
            pl.BlockSpec((None, 1, hv), lambda b, g, ci: (l, 0, 0)),
        ],
        out_specs=[
            pl.BlockSpec((c, hg * hv), lambda b, g, ci: (b * ncs + ci, g)),
            pl.BlockSpec((None, hg, hk, hv), lambda b, g, ci: (b, g, 0, 0)),
        ],
        out_shape=[jax.ShapeDtypeStruct((nb * t, nh * hv), BF16), jax.ShapeDtypeStruct((nb, nh, hk, hv), F32)],
        scratch_shapes=[pltpu.VMEM((hg, hk, hv), F32)],
        compiler_params=_params("parallel", "parallel", "arbitrary"),
        name="delta_prompt",
    )(qkv, qkv, qkv, gb, h_main, nw)


def _delta_sample_body(q_ref, k_ref, v_ref, gb_ref, z_ref, nw_ref, s_ref, all_states_ref, og_ref, so_ref,
                       kq_scr, p_scr, u_scr, kt_scr, e_scr, *, bs, ts, bb, hk, hv):
    del all_states_ref
    b0 = pl.multiple_of(pl.program_id(1) * bb, bb)
    half = V7X_LANES // 2
    pad = V7X_SUBLANES - ts

    def rows(ref, t):
        return ref[pl.ds(t * bs + b0, bb), :]

    ks = [rows(k_ref, t) for t in range(ts)]
    qs = [rows(q_ref, t) for t in range(ts)]
    for t in range(ts):
        kq_scr[t * bb:(t + 1) * bb, :] = ks[t]
        kq_scr[(ts + t) * bb:(ts + t + 1) * bb, :] = qs[t]

    def state_products(g, carry):
        seqs = [g * SAMPLE_GROUP + i for i in range(SAMPLE_GROUP)]
        kq = [kq_scr[pl.ds(b, 2 * ts, stride=bb), :].astype(BF16) for b in seqs]
        s16 = [s_ref[b].astype(BF16) for b in seqs]
        prod = [_dot(kq[i], s16[i]) for i in range(SAMPLE_GROUP)]
        for i, b in enumerate(seqs):
            p_scr[pl.ds(b, 2 * ts, stride=bb), :] = prod[i]
        return carry

    lax.fori_loop(0, bb // SAMPLE_GROUP, state_products, 0)

    gs = [rows(gb_ref, t)[:, 0:1] for t in range(ts)]
    betas = [rows(gb_ref, t)[:, half:half + 1] for t in range(ts)]
    us = []
    for t in range(ts):
        u = betas[t] * (rows(v_ref, t) - jnp.exp(gs[t]) * p_scr[t * bb:(t + 1) * bb, :])
        for s in range(t):
            kk = jnp.sum(ks[t] * ks[s], axis=-1, keepdims=True)
            u = u - (betas[t] * kk * jnp.exp(gs[t] - gs[s])) * us[s]
        us.append(u)
    for t in range(ts):
        o = jnp.exp(gs[t]) * p_scr[(ts + t) * bb:(ts + t + 1) * bb, :]
        for s in range(t + 1):
            qk = jnp.sum(qs[t] * ks[s], axis=-1, keepdims=True)
            o = o + (qk * jnp.exp(gs[t] - gs[s])) * us[s]
        og_ref[pl.ds(t * bs + b0, bb), :] = _gated_norm(o, rows(z_ref, t), nw_ref[...]).astype(og_ref.dtype)
        u_scr[t * bb:(t + 1) * bb, :] = us[t]
        kt_scr[t * bb:(t + 1) * bb, :] = ks[t] * jnp.exp(gs[ts - 1] - gs[t])
    u_scr[ts * bb:(ts + pad) * bb, :] = jnp.zeros((pad * bb, hv), F32)
    kt_scr[ts * bb:(ts + pad) * bb, :] = jnp.zeros((pad * bb, hk), F32)
    e_scr[...] = jnp.broadcast_to(jnp.exp(gs[ts - 1]), (bb, hv))

    def state_update(g, carry):
        seqs = [g * SAMPLE_GROUP + i for i in range(SAMPLE_GROUP)]
        ub = [u_scr[pl.ds(b, V7X_SUBLANES, stride=bb), :].astype(BF16) for b in seqs]
        kb = [kt_scr[pl.ds(b, V7X_SUBLANES, stride=bb), :].astype(BF16) for b in seqs]
        upd = [_tdot(kb[i], ub[i]) for i in range(SAMPLE_GROUP)]
        for i, b in enumerate(seqs):
            so_ref[b] = e_scr[pl.ds(b, 1), :] * s_ref[b] + upd[i]
        return carry

    lax.fori_loop(0, bb // SAMPLE_GROUP, state_update, 0)


def _delta_sample(qkv, gb, h_main, nw, state, new_states, l, col_z, row0, bs, ts, nh, hk, hv):
    tbs = bs * ts
    bb = _divisor(bs, 128, V7X_BF16_ROWS)
    assert row0 % tbs == 0 and col_z % hv == 0 and 2 * ts <= V7X_SUBLANES and bb % SAMPLE_GROUP == 0
    body = functools.partial(_delta_sample_body, bs=bs, ts=ts, bb=bb, hk=hk, hv=hv)

    def head_spec(sec):
        return pl.BlockSpec((None, tbs, hk), lambda h, g: (sec * nh + h, row0 // tbs, 0))

    return pl.pallas_call(
        body,
        grid=(nh, bs // bb),
        in_specs=[
            head_spec(0), head_spec(1), head_spec(2),
            pl.BlockSpec((None, tbs, V7X_LANES), lambda h, g: (h, row0 // tbs, 0)),
            pl.BlockSpec((tbs, hv), lambda h, g: (row0 // tbs, col_z // hv + h)),
            pl.BlockSpec((None, 1, hv), lambda h, g: (l, 0, 0)),
            pl.BlockSpec((None, bb, None, hk, hv), lambda h, g: (l, g, h, 0, 0)),
            pl.BlockSpec(memory_space=pl.ANY),
        ],
        out_specs=[
            pl.BlockSpec((tbs, hv), lambda h, g: (0, h)),
            pl.BlockSpec((None, bb, None, hk, hv), lambda h, g: (l, g, h, 0, 0)),
        ],
        out_shape=[jax.ShapeDtypeStruct((tbs, nh * hv), BF16), jax.ShapeDtypeStruct(new_states.shape, F32)],
        input_output_aliases={7: 1},
        scratch_shapes=[
            pltpu.VMEM((2 * ts * bb, hk), F32),
            pltpu.VMEM((2 * ts * bb, hv), F32),
            pltpu.VMEM((V7X_SUBLANES * bb, hv), F32),
            pltpu.VMEM((V7X_SUBLANES * bb, hk), F32),
            pltpu.VMEM((bb, hv), F32),
        ],
        compiler_params=_params("parallel", "arbitrary"),
        name="delta_sample",
    )(qkv, qkv, qkv, gb, h_main, nw, state, new_states)


def kernel(x_prompt, x_sample, state_pool, state_conv, state_delta, w_in, w_pool_grp, pool_scale, w_pool_up,
           conv_w, a_log, dt_bias, o_norm_w, w_delta_up, w_out, ln1_g, ln1_b, w_gate, w_up, w_down, ln2_g, ln2_b):
    nb, t, d = x_prompt.shape
    bs, ts, _ = x_sample.shape
    depth = w_in.shape[0]
    pb, dp = state_pool.shape[2], state_pool.shape[3]
    cwm1 = state_conv.shape[2]
    assert t >= pb and t >= cwm1
    dqkv = state_conv.shape[3]
    nh, hk, hv = state_delta.shape[2], state_delta.shape[3], state_delta.shape[4]
    dv = nh * hv
    dff = w_gate.shape[-1]
    assert dqkv == 2 * nh * hk + dv and hk == hv
    mp, ms = nb * t, bs * ts
    m = mp + ms
    tb = ms
    assert t % tb == 0 and tb % DELTA_CHUNK == 0
    steps_per_seq = t // tb
    n_prompt_steps = mp // tb
    alpha = float((2 * depth) ** 0.25)

    off_qkv = dp
    off_z = off_qkv + dqkv
    off_beta = off_z + dv
    off_ga = off_beta + 2 * nh
    off_gb = off_ga + d
    assert w_in.shape[-1] == off_gb + d
    col_qkv, col_z, col_pool = 0, dqkv, dqkv + dv
    col_ga = col_pool + dp
    col_gb = col_ga + d
    col_small = col_gb + d
    segments = ((off_qkv, col_qkv, dqkv), (off_z, col_z, dv), (0, col_pool, dp), (off_ga, col_ga, 2 * d),
                (off_beta, col_small, INPROJ_BLOCK_N))
    assert off_beta + INPROJ_BLOCK_N <= w_in.shape[-1]
    w_in_t = jnp.swapaxes(w_in, 1, 2)
    w_grp16 = w_pool_grp.astype(BF16)
    w_pu16 = w_pool_up.astype(BF16)
    w_du16 = w_delta_up.astype(BF16)

    lane_pad = ((0, 0), (nh, V7X_LANES - 2 * nh))
    alog_row = jnp.pad(a_log, lane_pad)[:, None, :]
    dtb_row = jnp.pad(dt_bias, lane_pad)[:, None, :]
    scale_row = pool_scale[:, None, :]
    nw_row = o_norm_w[:, None, :]
    ln1_g, ln1_b, ln2_g, ln2_b = (a[:, None, :] for a in (ln1_g, ln1_b, ln2_g, ln2_b))
    st_pool_tm = jnp.swapaxes(state_pool, 1, 2)
    st_conv_tm = jnp.swapaxes(state_conv, 1, 2)

    x = jnp.concatenate([x_prompt.reshape(mp, d), jnp.swapaxes(x_sample, 0, 1).reshape(ms, d)], axis=0)
    x16 = x.astype(BF16)

    pool_p, conv_p, delta_p, pool_s, conv_s = [], [], [], [], []
    new_delta_s = jnp.zeros(state_delta.shape, F32)
    for l in range(depth):
        h_main = _inproj(x16, w_in_t, segments, l)
        yp = _pool(h_main, st_pool_tm, w_grp16, scale_row, l, col_pool, tb, bs, ts, n_prompt_steps, steps_per_seq)
        qkv, gb = _prep(h_main, st_conv_tm, conv_w, alog_row, dtb_row, l, col_qkv, col_small, tb, bs, ts, nh, hk,
                        n_prompt_steps, steps_per_seq)
        og_p, s_p = _delta_prompt(qkv, gb, h_main, nw_row, l, col_z, nb, t, nh, hk, hv, hg=min(nh, 16))
        og_s, new_delta_s = _delta_sample(qkv, gb, h_main, nw_row, state_delta, new_delta_s, l, col_z, mp, bs, ts,
                                          nh, hk, hv)
        merged = _merge(yp, og_p, og_s, w_pu16, w_du16, h_main, l, col_ga, col_gb, tb)
        r1 = _outproj(merged, w_out, x, l, alpha)
        x1, x1_16 = _layer_norm(r1, ln1_g, ln1_b, l)
        act, w_down16 = _glu(x1_16, w_gate, w_up, w_down, l)
        r2 = _down(act, w_down16, x1, alpha)
        if l + 1 < depth:
            x, x16 = _layer_norm(r2, ln2_g, ln2_b, l)
        else:
            y_p, y_s = _layer_norm_split(r2, ln2_g, ln2_b, l, mp)

        def tail_rows(col, width, keep):
            return jnp.stack([h_main[(b + 1) * t - keep:(b + 1) * t, col:col + width] for b in range(nb)])

        def sample_rows(col, width):
            return jnp.swapaxes(h_main[mp:, col:col + width].reshape(ts, bs, width), 0, 1)

        pool_p.append(tail_rows(col_pool, dp, pb))
        conv_p.append(tail_rows(col_qkv, dqkv, cwm1))
        pool_s.append(jnp.concatenate([state_pool[l], sample_rows(col_pool, dp)], axis=1)[:, -pb:])
        conv_s.append(jnp.concatenate([state_conv[l], sample_rows(col_qkv, dqkv)], axis=1)[:, -cwm1:])
        delta_p.append(s_p)

    y_prompt = y_p.reshape(nb, t, d)
    y_sample = jnp.swapaxes(y_s.reshape(ts, bs, d), 0, 1)
    return (y_prompt, y_sample, jnp.stack(pool_p), jnp.stack(conv_p), jnp.stack(delta_p),
            jnp.stack(pool_s), jnp.stack(conv_s), new_delta_s)
```

```python
import functools
import math

import jax
import jax.numpy as jnp
from jax import lax
from jax.experimental import pallas as pl
from jax.experimental.pallas import tpu as pltpu

F32 = jnp.float32
BF16 = jnp.bfloat16

POOL_WINDOWS = (2, 4, 8, 16)
DELTA_CHUNK = 64
PAST_LEN = 16384
LN_EPS = 1e-5
RMS_EPS = 1e-6

V7X_LANES = 128
V7X_SUBLANES = 8
V7X_BF16_ROWS = 16
V7X_VMEM_LIMIT_BYTES = 56 * 1024 * 1024
POOL_HALO = 16
CONV_HALO = V7X_SUBLANES

ROWS_DOUBLE_BUFFERED = 1088
ROWS_SINGLE_BUFFERED = 2176
ROWS_LAYER_NORM = 272
GLU_BLOCK_N = 256
INPROJ_BLOCK_N = 512
OUT_BLOCK_N = 512
MERGE_BLOCK_N = 1024
DOWN_BLOCK_K = 5504
SAMPLE_BLOCK_SEQS = 128
SAMPLE_GROUP = 8


def _params(*semantics):
    return pltpu.CompilerParams(dimension_semantics=semantics, vmem_limit_bytes=V7X_VMEM_LIMIT_BYTES)


def _divisor(n, target, mult):
    best = None
    for d in range(mult, min(n, target) + 1, mult):
        if n % d == 0:
            best = d
    assert best is not None, (n, target, mult)
    return best


def _dot(a, b):
    return jnp.dot(a, b, preferred_element_type=F32)


def _dot_t(a, b):
    return lax.dot_general(a, b, (((1,), (1,)), ((), ())), preferred_element_type=F32)


def _tdot(a, b):
    return lax.dot_general(a, b, (((0,), (0,)), ((), ())), preferred_element_type=F32)


def _softplus(x):
    return jnp.maximum(x, 0.0) + jnp.log1p(jnp.exp(-jnp.abs(x)))


def _inproj_body(x_ref, w_ref, o_ref):
    x = x_ref[...]
    o_ref[...] = _dot(x, w_ref[0].astype(x.dtype).T)


def _inproj(x, w_in_t, segments, l):
    M, K = x.shape
    n_out = sum(width for _, _, width in segments)
    bm = _divisor(M, ROWS_SINGLE_BUFFERED, V7X_BF16_ROWS)
    bn = INPROJ_BLOCK_N
    assert all(src % V7X_SUBLANES == 0 and dst % bn == 0 and width % bn == 0 for src, dst, width in segments)

    def src_row(j):
        col = j * bn
        row = 0
        for src, dst, width in segments:
            row = jnp.where((col >= dst) & (col < dst + width), src + col - dst, row)
        return pl.multiple_of(row, V7X_SUBLANES)

    return pl.pallas_call(
        _inproj_body,
        grid=(M // bm, n_out // bn),
        in_specs=[
            pl.BlockSpec((bm, K), lambda i, j: (i, 0), pipeline_mode=pl.Buffered(1)),
            pl.BlockSpec((pl.Element(1), pl.Element(bn), pl.Element(K)), lambda i, j: (l, src_row(j), 0)),
        ],
        out_specs=pl.BlockSpec((bm, bn), lambda i, j: (i, j)),
        out_shape=jax.ShapeDtypeStruct((M, n_out), F32),
        compiler_params=_params("parallel", "arbitrary"),
        name="inproj",
    )(x, w_in_t)


def _merge_body(yp_ref, ogp_ref, ogs_ref, wpu_ref, wdu_ref, ga_ref, gb_ref, o_ref, *, n_prompt_blocks):
    og = jnp.where(pl.program_id(1) < n_prompt_blocks, ogp_ref[...], ogs_ref[...])
    ya = _dot(yp_ref[...], wpu_ref[...].astype(og.dtype))
    yb = _dot(og, wdu_ref[...].astype(og.dtype))
    o_ref[...] = (jax.nn.sigmoid(ga_ref[...]) * ya + jax.nn.sigmoid(gb_ref[...]) * yb).astype(o_ref.dtype)


def _merge(yp, og_p, og_s, w_pool_up, w_delta_up, h_main, l, col_ga, col_gb, tb):
    M, DP = yp.shape
    DV = og_p.shape[1]
    D = w_pool_up.shape[-1]
    bn = _divisor(math.gcd(D, col_ga, col_gb), MERGE_BLOCK_N, V7X_LANES)
    npb = og_p.shape[0] // tb
    assert og_s.shape[0] == tb and col_ga % bn == 0 and col_gb % bn == 0
    return pl.pallas_call(
        functools.partial(_merge_body, n_prompt_blocks=npb),
        grid=(D // bn, M // tb),
        in_specs=[
            pl.BlockSpec((tb, DP), lambda j, i: (i, 0)),
            pl.BlockSpec((tb, DV), lambda j, i: (jnp.minimum(i, npb - 1), 0)),
            pl.BlockSpec((tb, DV), lambda j, i: (0, 0), pipeline_mode=pl.Buffered(1)),
            pl.BlockSpec((None, DP, bn), lambda j, i: (l, 0, j), pipeline_mode=pl.Buffered(1)),
            pl.BlockSpec((None, DV, bn), lambda j, i: (l, 0, j), pipeline_mode=pl.Buffered(1)),
            pl.BlockSpec((tb, bn), lambda j, i: (i, col_ga // bn + j)),
            pl.BlockSpec((tb, bn), lambda j, i: (i, col_gb // bn + j)),
        ],
        out_specs=pl.BlockSpec((tb, bn), lambda j, i: (i, j)),
        out_shape=jax.ShapeDtypeStruct((M, D), BF16),
        compiler_params=_params("parallel", "parallel"),
        name="merge",
    )(yp, og_p, og_s, w_pool_up, w_delta_up, h_main, h_main)


def _outproj_body(a_ref, w_ref, x_ref, o_ref, *, alpha):
    a = a_ref[...]
    o_ref[...] = alpha * x_ref[...] + _dot(a, w_ref[...].astype(a.dtype))


def _outproj(a, w, x, l, alpha):
    M, K = a.shape
    N = w.shape[-1]
    bm = _divisor(M, ROWS_DOUBLE_BUFFERED, V7X_BF16_ROWS)
    bn = _divisor(N, OUT_BLOCK_N, V7X_LANES)
    return pl.pallas_call(
        functools.partial(_outproj_body, alpha=alpha),
        grid=(M // bm, N // bn),
        in_specs=[
            pl.BlockSpec((bm, K), lambda i, j: (i, 0)),
            pl.BlockSpec((None, K, bn), lambda i, j: (l, 0, j)),
            pl.BlockSpec((bm, bn), lambda i, j: (i, j)),
        ],
        out_specs=pl.BlockSpec((bm, bn), lambda i, j: (i, j)),
        out_shape=jax.ShapeDtypeStruct((M, N), F32),
        compiler_params=_params("parallel", "parallel"),
        name="outproj",
    )(a, w, x)


def _glu_body(x_ref, wg_ref, wu_ref, wd_ref, o_ref, wd16_ref, *, n_cast_steps):
    x = x_ref[...]
    g = _dot(x, wg_ref[...].astype(x.dtype))
    u = _dot(x, wu_ref[...].astype(x.dtype))
    o_ref[...] = (g * jax.nn.sigmoid(g) * u).astype(o_ref.dtype)

    @pl.when(pl.program_id(0) * pl.num_programs(1) + pl.program_id(1) < n_cast_steps)
    def _():
        wd16_ref[...] = wd_ref[...].astype(wd16_ref.dtype)


def _glu(x, w_gate, w_up, w_down, l):
    M, K = x.shape
    N = w_gate.shape[-1]
    F, D = w_down.shape[1], w_down.shape[2]
    bm = _divisor(M, ROWS_SINGLE_BUFFERED, V7X_BF16_ROWS)
    bn = _divisor(N, GLU_BLOCK_N, V7X_LANES)
    nj = N // bn
    steps = (M // bm) * nj
    slab = min(s for s in range(V7X_BF16_ROWS, F + 1, V7X_BF16_ROWS) if F % s == 0 and s >= 128 and F // s <= steps)
    n_cast_steps = F // slab

    def slab_index(i, j):
        return jnp.minimum(i * nj + j, n_cast_steps - 1)

    return pl.pallas_call(
        functools.partial(_glu_body, n_cast_steps=n_cast_steps),
        grid=(M // bm, nj),
        in_specs=[
            pl.BlockSpec((bm, K), lambda i, j: (i, 0), pipeline_mode=pl.Buffered(1)),
            pl.BlockSpec((None, K, bn), lambda i, j: (l, 0, j)),
            pl.BlockSpec((None, K, bn), lambda i, j: (l, 0, j)),
            pl.BlockSpec((None, slab, D), lambda i, j: (l, slab_index(i, j), 0)),
        ],
        out_specs=[
            pl.BlockSpec((bm, bn), lambda i, j: (i, j)),
            pl.BlockSpec((slab, D), lambda i, j: (slab_index(i, j), 0)),
        ],
        out_shape=[jax.ShapeDtypeStruct((M, N), BF16), jax.ShapeDtypeStruct((F, D), BF16)],
        compiler_params=_params("arbitrary", "arbitrary"),
        name="glu",
    )(x, w_gate, w_up, w_down)


def _down_body(a_ref, w_ref, x_ref, o_ref, *, alpha):
    @pl.when(pl.program_id(2) == 0)
    def _():
        o_ref[...] = alpha * x_ref[...]

    o_ref[...] += _dot(a_ref[...], w_ref[...])


def _down(a, w, x, alpha):
    M, K = a.shape
    N = w.shape[-1]
    bm = _divisor(M, ROWS_DOUBLE_BUFFERED, V7X_BF16_ROWS)
    bn = _divisor(N, OUT_BLOCK_N, V7X_LANES)
    tk = _divisor(K, DOWN_BLOCK_K, V7X_LANES)
    return pl.pallas_call(
        functools.partial(_down_body, alpha=alpha),
        grid=(M // bm, N // bn, K // tk),
        in_specs=[
            pl.BlockSpec((bm, tk), lambda i, j, k: (i, k)),
            pl.BlockSpec((tk, bn), lambda i, j, k: (k, j)),
            pl.BlockSpec((bm, bn), lambda i, j, k: (i, j)),
        ],
        out_specs=pl.BlockSpec((bm, bn), lambda i, j, k: (i, j)),
        out_shape=jax.ShapeDtypeStruct((M, N), F32),
        compiler_params=_params("parallel", "parallel", "arbitrary"),
        name="down",
    )(a, w, x)


def _ln_body(r_ref, g_ref, b_ref, xf_ref, xb_ref):
    r = r_ref[...]
    mu = jnp.mean(r, axis=-1, keepdims=True)
    xc = r - mu
    var = jnp.mean(xc * xc, axis=-1, keepdims=True)
    y = xc * lax.rsqrt(var + LN_EPS) * g_ref[...] + b_ref[...]
    xf_ref[...] = y
    xb_ref[...] = y.astype(xb_ref.dtype)


def _ln_split_body(r_ref, g_ref, b_ref, yp_ref, ys_ref, *, n_prompt_blocks):
    r = r_ref[...]
    mu = jnp.mean(r, axis=-1, keepdims=True)
    xc = r - mu
    var = jnp.mean(xc * xc, axis=-1, keepdims=True)
    y = xc * lax.rsqrt(var + LN_EPS) * g_ref[...] + b_ref[...]

    @pl.when(pl.program_id(0) < n_prompt_blocks)
    def _():
        yp_ref[...] = y

    @pl.when(pl.program_id(0) >= n_prompt_blocks)
    def _():
        ys_ref[...] = y


def _layer_norm_split(r, g, b, l, mp):
    M, D = r.shape
    ms = M - mp
    bm = _divisor(math.gcd(mp, ms), ROWS_LAYER_NORM, V7X_SUBLANES)
    npb = mp // bm
    return pl.pallas_call(
        functools.partial(_ln_split_body, n_prompt_blocks=npb),
        grid=(M // bm,),
        in_specs=[
            pl.BlockSpec((bm, D), lambda i: (i, 0)),
            pl.BlockSpec((None, 1, D), lambda i: (l, 0, 0)),
            pl.BlockSpec((None, 1, D), lambda i: (l, 0, 0)),
        ],
        out_specs=[
            pl.BlockSpec((bm, D), lambda i: (jnp.minimum(i, npb - 1), 0)),
            pl.BlockSpec((bm, D), lambda i: (jnp.maximum(i - npb, 0), 0)),
        ],
        out_shape=[jax.ShapeDtypeStruct((mp, D), F32), jax.ShapeDtypeStruct((ms, D), F32)],
        compiler_params=_params("arbitrary"),
        name="layer_norm_split",
    )(r, g, b)


def _layer_norm(r, g, b, l):
    M, D = r.shape
    bm = _divisor(M, ROWS_LAYER_NORM, V7X_BF16_ROWS)
    return pl.pallas_call(
        _ln_body,
        grid=(M // bm,),
        in_specs=[
            pl.BlockSpec((bm, D), lambda i: (i, 0)),
            pl.BlockSpec((None, 1, D), lambda i: (l, 0, 0)),
            pl.BlockSpec((None, 1, D), lambda i: (l, 0, 0)),
        ],
        out_specs=[pl.BlockSpec((bm, D), lambda i: (i, 0)), pl.BlockSpec((bm, D), lambda i: (i, 0))],
        out_shape=[jax.ShapeDtypeStruct((M, D), F32), jax.ShapeDtypeStruct((M, D), BF16)],
        compiler_params=_params("parallel"),
        name="layer_norm",
    )(r, g, b)


def _pool_body(u_ref, st_ref, wg_ref, sc_ref, o_ref, ext_ref, *, tb, bs, ts, pb, gw, n_prompt_steps, steps_per_seq):
    step = pl.program_id(0)

    def finish(mixed):
        for gi, m in enumerate(mixed):
            cols = slice(gi * gw, (gi + 1) * gw)
            y = _dot(m.astype(BF16), wg_ref[gi])
            o_ref[:, cols] = (y * sc_ref[:, cols]).astype(o_ref.dtype)

    @pl.when(step < n_prompt_steps)
    def _():
        blk = step % steps_per_seq

        @pl.when(blk == 0)
        def _():
            ext_ref[0:POOL_HALO, :] = jnp.zeros((POOL_HALO, ext_ref.shape[1]), F32)

        ext_ref[POOL_HALO:POOL_HALO + tb, :] = u_ref[...]
        pos = blk * tb + lax.broadcasted_iota(jnp.int32, (tb, 1), 0)
        mixed = []
        for gi, w in enumerate(POOL_WINDOWS):
            cols = slice(gi * gw, (gi + 1) * gw)
            cur = ext_ref[POOL_HALO:POOL_HALO + tb, cols]
            win = cur
            for i in range(1, w):
                win = win + ext_ref[POOL_HALO - i:POOL_HALO - i + tb, cols]
            cnt = jnp.minimum(pos + 1, w).astype(F32)
            mixed.append(win / cnt - cur)
        finish(mixed)
        ext_ref[0:POOL_HALO, :] = ext_ref[tb:tb + POOL_HALO, :]

    @pl.when(step >= n_prompt_steps)
    def _():
        mixed = []
        for gi, w in enumerate(POOL_WINDOWS):
            cols = slice(gi * gw, (gi + 1) * gw)
            per_t = []
            for t in range(ts):
                acc = None
                for i in range(w):
                    e = pb + t - i
                    term = st_ref[e, :, cols] if e < pb else u_ref[(e - pb) * bs:(e - pb + 1) * bs, cols]
                    acc = term if acc is None else acc + term
                cnt = float(min(PAST_LEN + t + 1, w))
                per_t.append(acc / cnt - u_ref[t * bs:(t + 1) * bs, cols])
            mixed.append(jnp.concatenate(per_t, axis=0))
        finish(mixed)


def _pool(h_main, st_pool_tm, w_grp, scale, l, col_pool, tb, bs, ts, n_prompt_steps, steps_per_seq):
    M = h_main.shape[0]
    _, pb, _, DP = st_pool_tm.shape
    G, gw = w_grp.shape[1], w_grp.shape[2]
    assert G == len(POOL_WINDOWS) and pb == max(POOL_WINDOWS) - 1 and pb <= POOL_HALO and col_pool % DP == 0
    body = functools.partial(_pool_body, tb=tb, bs=bs, ts=ts, pb=pb, gw=gw,
                             n_prompt_steps=n_prompt_steps, steps_per_seq=steps_per_seq)
    return pl.pallas_call(
        body,
        grid=(M // tb,),
        in_specs=[
            pl.BlockSpec((tb, DP), lambda s: (s, col_pool // DP)),
            pl.BlockSpec((None, pb, bs, DP), lambda s: (l, 0, 0, 0)),
            pl.BlockSpec((None, G, gw, gw), lambda s: (l, 0, 0, 0)),
            pl.BlockSpec((None, 1, DP), lambda s: (l, 0, 0)),
        ],
        out_specs=pl.BlockSpec((tb, DP), lambda s: (s, 0)),
        out_shape=jax.ShapeDtypeStruct((M, DP), BF16),
        scratch_shapes=[pltpu.VMEM((POOL_HALO + tb, DP), F32)],
        compiler_params=_params("arbitrary"),
        name="pool_mixer",
    )(h_main, st_pool_tm, w_grp, scale)


def _prep_body(x_ref, hs_ref, stc_ref, cw_ref, al_ref, dtb_ref, qkv_ref, gb_ref, ext_ref, halo_ref, *,
               tb, bs, ts, cw, nh, hk, chunk, n_prompt_steps, steps_per_seq):
    step = pl.program_id(0)
    sec = pl.program_id(1)
    half = V7X_LANES // 2

    def emit(conv_head):
        @pl.when(sec < 2)
        def _():
            qscale = jnp.where(sec == 0, hk ** -0.5, 1.0).astype(F32)
            for h in range(nh):
                y = conv_head(h)
                y = y * jax.nn.sigmoid(y)
                ss = jnp.sum(y * y, axis=-1, keepdims=True)
                qkv_ref[h] = y * (lax.rsqrt(ss + RMS_EPS) * qscale)

        @pl.when(sec == 2)
        def _():
            for h in range(nh):
                y = conv_head(h)
                qkv_ref[h] = y * jax.nn.sigmoid(y)

    def emit_gates(gcum, beta):
        for h in range(nh):
            gcol = jnp.broadcast_to(gcum[:, nh + h:nh + h + 1], (tb, half))
            bcol = jnp.broadcast_to(beta[:, h:h + 1], (tb, half))
            gb_ref[h] = jnp.concatenate([gcol, bcol], axis=1)

    def gate_values():
        hs = hs_ref[...]
        beta = jax.nn.sigmoid(hs)
        g = -jnp.exp(al_ref[...]) * _softplus(hs + dtb_ref[...])
        return g, beta

    @pl.when(step < n_prompt_steps)
    def _():
        blk = step % steps_per_seq

        @pl.when(blk == 0)
        def _():
            ext_ref[0:CONV_HALO, :] = jnp.zeros((CONV_HALO, ext_ref.shape[1]), F32)

        @pl.when(blk != 0)
        def _():
            ext_ref[0:CONV_HALO, :] = halo_ref[sec]

        ext_ref[CONV_HALO:CONV_HALO + tb, :] = x_ref[...]
        halo_ref[sec] = ext_ref[tb:tb + CONV_HALO, :]
        base = CONV_HALO - (cw - 1)

        def conv_head(h):
            cols = slice(h * hk, (h + 1) * hk)
            y = ext_ref[base:base + tb, cols] * cw_ref[0:1, cols]
            for i in range(1, cw):
                y = y + ext_ref[base + i:base + i + tb, cols] * cw_ref[i:i + 1, cols]
            return y

        emit(conv_head)

        @pl.when(sec == 0)
        def _():
            g, beta = gate_values()
            t_in = lax.broadcasted_iota(jnp.int32, g.shape, 0) % chunk
            d = 1
            while d < chunk:
                g = g + jnp.where(t_in >= d, pltpu.roll(g, d, 0), 0.0)
                d *= 2
            emit_gates(g, beta)

    @pl.when(step >= n_prompt_steps)
    def _():
        def ext_slab(j, cols):
            return stc_ref[j, :, cols] if j < cw - 1 else x_ref[(j - (cw - 1)) * bs:(j - (cw - 2)) * bs, cols]

        def conv_head(h):
            cols = slice(h * hk, (h + 1) * hk)
            per_t = []
            for t in range(ts):
                y = ext_slab(t, cols) * cw_ref[0:1, cols]
                for i in range(1, cw):
                    y = y + ext_slab(t + i, cols) * cw_ref[i:i + 1, cols]
                per_t.append(y)
            return jnp.concatenate(per_t, axis=0)

        emit(conv_head)

        @pl.when(sec == 0)
        def _():
            g, beta = gate_values()
            slabs = [g[0:bs]]
            for t in range(1, ts):
                slabs.append(slabs[-1] + g[t * bs:(t + 1) * bs])
            emit_gates(jnp.concatenate(slabs, axis=0), beta)


def _prep(h_main, st_conv_tm, conv_w, alog_row, dtb_row, l, col_qkv, col_small, tb, bs, ts, nh, hk,
          n_prompt_steps, steps_per_seq):
    M = h_main.shape[0]
    cw = conv_w.shape[1]
    sw = nh * hk
    assert cw - 1 <= CONV_HALO and col_qkv % sw == 0 and 2 * nh <= V7X_LANES and col_small % V7X_LANES == 0
    body = functools.partial(_prep_body, tb=tb, bs=bs, ts=ts, cw=cw, nh=nh, hk=hk, chunk=DELTA_CHUNK,
                             n_prompt_steps=n_prompt_steps, steps_per_seq=steps_per_seq)
    return pl.pallas_call(
        body,
        grid=(M // tb, 3),
        in_specs=[
            pl.BlockSpec((tb, sw), lambda s, c: (s, col_qkv // sw + c)),
            pl.BlockSpec((tb, V7X_LANES), lambda s, c: (s, col_small // V7X_LANES)),
            pl.BlockSpec((None, cw - 1, bs, sw), lambda s, c: (l, 0, 0, c)),
            pl.BlockSpec((None, cw, sw), lambda s, c: (l, 0, c)),
            pl.BlockSpec((None, 1, V7X_LANES), lambda s, c: (l, 0, 0)),
            pl.BlockSpec((None, 1, V7X_LANES), lambda s, c: (l, 0, 0)),
        ],
        out_specs=[
            pl.BlockSpec((nh, tb, hk), lambda s, c: (c, s, 0)),
            pl.BlockSpec((nh, tb, V7X_LANES), lambda s, c: (0, s, 0)),
        ],
        out_shape=[jax.ShapeDtypeStruct((3 * nh, M, hk), F32), jax.ShapeDtypeStruct((nh, M, V7X_LANES), F32)],
        scratch_shapes=[pltpu.VMEM((CONV_HALO + tb, sw), F32), pltpu.VMEM((3, CONV_HALO, sw), F32)],
        compiler_params=_params("arbitrary", "arbitrary"),
        name="delta_prep",
    )(h_main, h_main, st_conv_tm, conv_w, alog_row, dtb_row)


def _gated_norm(o, z, nw):
    o = o * lax.rsqrt(jnp.mean(o * o, axis=-1, keepdims=True) + RMS_EPS) * nw
    return o * (z * jax.nn.sigmoid(z))


def _delta_prompt_body(q_ref, k_ref, v_ref, gb_ref, z_ref, nw_ref, og_ref, so_ref, s_ref, *, hg, c, hk, hv):
    ci = pl.program_id(2)
    half = V7X_LANES // 2

    @pl.when(ci == 0)
    def _():
        s_ref[...] = jnp.zeros(s_ref.shape, F32)

    ti = lax.broadcasted_iota(jnp.int32, (c, c), 0)
    si = lax.broadcasted_iota(jnp.int32, (c, c), 1)
    incl = ti >= si
    strict = ti > si
    heads = range(hg)
    q = [q_ref[j] for j in heads]
    k = [k_ref[j] for j in heads]
    gbv = [gb_ref[j] for j in heads]
    g_col = [x[:, 0:1] for x in gbv]
    b_col = [x[:, half:half + 1] for x in gbv]
    decay = [jnp.where(incl, jnp.exp(jnp.where(incl, x[:, 0:c] - jnp.concatenate([x, x], axis=0).T[0:c, 0:c], 0.0)), 0.0)
             for x in gbv]
    kb = [k[j] * b_col[j] for j in heads]
    k16 = [x.astype(BF16) for x in k]
    a = [jnp.where(strict, _dot_t(kb[j].astype(BF16), k16[j]) * decay[j], 0.0) for j in heads]
    qk = [_dot_t(q[j].astype(BF16), k16[j]) * decay[j] for j in heads]
    n = [-x for x in a]
    p = [_dot(x.astype(BF16), x.astype(BF16)) for x in a]
    span = 4
    while span < c:
        both = [_dot(jnp.concatenate([n[j], p[j]], axis=0).astype(BF16), p[j].astype(BF16)) for j in heads]
        n = [n[j] + p[j] + both[j][0:c] for j in heads]
        p = [both[j][c:2 * c] for j in heads]
        span *= 2
    n = [n[j] + p[j] + _dot(n[j].astype(BF16), p[j].astype(BF16)) for j in heads]
    eg = [jnp.exp(x) for x in g_col]
    rhs = [jnp.concatenate([kb[j] * eg[j], v_ref[j] * b_col[j]], axis=1) for j in heads]
    tr = [rhs[j] + _dot(n[j].astype(BF16), rhs[j].astype(BF16)) for j in heads]
    g_last = [x[c - 1:c, :] for x in g_col]
    k_tail = [(k[j] * jnp.exp(g_last[j] - g_col[j])).astype(BF16) for j in heads]
    s0 = [s_ref[j] for j in heads]
    ps = [_dot(jnp.concatenate([tr[j][:, 0:hk], q[j] * eg[j]], axis=0).astype(BF16), s0[j].astype(BF16)) for j in heads]
    u = [(tr[j][:, hk:hk + hv] - ps[j][0:c]).astype(BF16) for j in heads]
    for j in heads:
        s_ref[j] = jnp.exp(g_last[j]) * s0[j] + _tdot(k_tail[j], u[j])
    o = [ps[j][c:2 * c] + _dot(qk[j].astype(BF16), u[j]) for j in heads]
    for j in heads:
        cols = slice(j * hv, (j + 1) * hv)
        og_ref[:, cols] = _gated_norm(o[j], z_ref[:, cols], nw_ref[...]).astype(og_ref.dtype)

    @pl.when(ci == pl.num_programs(2) - 1)
    def _():
        so_ref[...] = s_ref[...]


def _delta_prompt(qkv, gb, h_main, nw, l, col_z, nb, t, nh, hk, hv, hg):
    c = DELTA_CHUNK
    assert t % c == 0 and nh % hg == 0 and col_z % (hg * hv) == 0 and hk == V7X_LANES and c <= V7X_LANES // 2
    ncs = t // c
    ngr = nh // hg
    body = functools.partial(_delta_prompt_body, hg=hg, c=c, hk=hk, hv=hv)

    def head_spec(sec):
        return pl.BlockSpec((hg, c, hk), lambda b, g, ci: (sec * ngr + g, b * ncs + ci, 0))

    return pl.pallas_call(
        body,
        grid=(nb, ngr, ncs),
        in_specs=[
            head_spec(0), head_spec(1), head_spec(2),
            pl.BlockSpec((hg, c, V7X_LANES), lambda b, g, ci: (g, b * ncs + ci, 0)),
            pl.BlockSpec((c, hg * hv), lambda b, g, ci: (b * ncs + ci, col_z // (hg * hv) + g)),
            pl.BlockSpec((None, 1, hv), lambda b, g, ci: (l, 0, 0)),
        ],
        out_specs=[
            pl.BlockSpec((c, hg * hv), lambda b, g, ci: (b * ncs + ci, g)),
            pl.BlockSpec((None, hg, hk, hv), lambda b, g, ci: (b, g, 0, 0)),
        ],
        out_shape=[jax.ShapeDtypeStruct((nb * t, nh * hv), BF16), jax.ShapeDtypeStruct((nb, nh, hk, hv), F32)],
        scratch_shapes=[pltpu.VMEM((hg, hk, hv), F32)],
        compiler_params=_params("parallel", "parallel", "arbitrary"),
        name="delta_prompt",
    )(qkv, qkv, qkv, gb, h_main, nw)


def _delta_sample_body(q_ref, k_ref, v_ref, gb_ref, z_ref, nw_ref, s_ref, all_states_ref, og_ref, so_ref,
                       kq_scr, p_scr, u_scr, kt_scr, e_scr, *, bs, ts, bb, hk, hv):
    del all_states_ref
    b0 = pl.multiple_of(pl.program_id(1) * bb, bb)
    half = V7X_LANES // 2
    pad = V7X_SUBLANES - ts

    def rows(ref, t):
        return ref[pl.ds(t * bs + b0, bb), :]

    ks = [rows(k_ref, t) for t in range(ts)]
    qs = [rows(q_ref, t) for t in range(ts)]
    for t in range(ts):
        kq_scr[t * bb:(t + 1) * bb, :] = ks[t]
        kq_scr[(ts + t) * bb:(ts + t + 1) * bb, :] = qs[t]

    def state_products(g, carry):
        seqs = [g * SAMPLE_GROUP + i for i in range(SAMPLE_GROUP)]
        kq = [kq_scr[pl.ds(b, 2 * ts, stride=bb), :].astype(BF16) for b in seqs]
        s16 = [s_ref[b].astype(BF16) for b in seqs]
        prod = [_dot(kq[i], s16[i]) for i in range(SAMPLE_GROUP)]
        for i, b in enumerate(seqs):
            p_scr[pl.ds(b, 2 * ts, stride=bb), :] = prod[i]
        return carry

    lax.fori_loop(0, bb // SAMPLE_GROUP, state_products, 0)

    gs = [rows(gb_ref, t)[:, 0:1] for t in range(ts)]
    betas = [rows(gb_ref, t)[:, half:half + 1] for t in range(ts)]
    us = []
    for t in range(ts):
        u = betas[t] * (rows(v_ref, t) - jnp.exp(gs[t]) * p_scr[t * bb:(t + 1) * bb, :])
        for s in range(t):
            kk = jnp.sum(ks[t] * ks[s], axis=-1, keepdims=True)
            u = u - (betas[t] * kk * jnp.exp(gs[t] - gs[s])) * us[s]
        us.append(u)
    for t in range(ts):
        o = jnp.exp(gs[t]) * p_scr[(ts + t) * bb:(ts + t + 1) * bb, :]
        for s in range(t + 1):
            qk = jnp.sum(qs[t] * ks[s], axis=-1, keepdims=True)
            o = o + (qk * jnp.exp(gs[t] - gs[s])) * us[s]
        og_ref[pl.ds(t * bs + b0, bb), :] = _gated_norm(o, rows(z_ref, t), nw_ref[...]).astype(og_ref.dtype)
        u_scr[t * bb:(t + 1) * bb, :] = us[t]
        kt_scr[t * bb:(t + 1) * bb, :] = ks[t] * jnp.exp(gs[ts - 1] - gs[t])
    u_scr[ts * bb:(ts + pad) * bb, :] = jnp.zeros((pad * bb, hv), F32)
    kt_scr[ts * bb:(ts + pad) * bb, :] = jnp.zeros((pad * bb, hk), F32)
    e_scr[...] = jnp.broadcast_to(jnp.exp(gs[ts - 1]), (bb, hv))

    def state_update(g, carry):
        seqs = [g * SAMPLE_GROUP + i for i in range(SAMPLE_GROUP)]
        ub = [u_scr[pl.ds(b, V7X_SUBLANES, stride=bb), :].astype(BF16) for b in seqs]
        kb = [kt_scr[pl.ds(b, V7X_SUBLANES, stride=bb), :].astype(BF16) for b in seqs]
        upd = [_tdot(kb[i], ub[i]) for i in range(SAMPLE_GROUP)]
        for i, b in enumerate(seqs):
            so_ref[b] = e_scr[pl.ds(b, 1), :] * s_ref[b] + upd[i]
        return carry

    lax.fori_loop(0, bb // SAMPLE_GROUP, state_update, 0)


def _delta_sample(qkv, gb, h_main, nw, state, new_states, l, col_z, row0, bs, ts, nh, hk, hv):
    tbs = bs * ts
    bb = _divisor(bs, SAMPLE_BLOCK_SEQS, V7X_BF16_ROWS)
    assert row0 % tbs == 0 and col_z % hv == 0 and 2 * ts <= V7X_SUBLANES and bb % SAMPLE_GROUP == 0
    body = functools.partial(_delta_sample_body, bs=bs, ts=ts, bb=bb, hk=hk, hv=hv)

    def head_spec(sec):
        return pl.BlockSpec((None, tbs, hk), lambda h, g: (sec * nh + h, row0 // tbs, 0))

    return pl.pallas_call(
        body,
        grid=(nh, bs // bb),
        in_specs=[
            head_spec(0), head_spec(1), head_spec(2),
            pl.BlockSpec((None, tbs, V7X_LANES), lambda h, g: (h, row0 // tbs, 0)),
            pl.BlockSpec((tbs, hv), lambda h, g: (row0 // tbs, col_z // hv + h)),
            pl.BlockSpec((None, 1, hv), lambda h, g: (l, 0, 0)),
            pl.BlockSpec((None, bb, None, hk, hv), lambda h, g: (l, g, h, 0, 0)),
            pl.BlockSpec(memory_space=pl.ANY),
        ],
        out_specs=[
            pl.BlockSpec((tbs, hv), lambda h, g: (0, h)),
            pl.BlockSpec((None, bb, None, hk, hv), lambda h, g: (l, g, h, 0, 0)),
        ],
        out_shape=[jax.ShapeDtypeStruct((tbs, nh * hv), BF16), jax.ShapeDtypeStruct(new_states.shape, F32)],
        input_output_aliases={7: 1},
        scratch_shapes=[
            pltpu.VMEM((2 * ts * bb, hk), F32),
            pltpu.VMEM((2 * ts * bb, hv), F32),
            pltpu.VMEM((V7X_SUBLANES * bb, hv), F32),
            pltpu.VMEM((V7X_SUBLANES * bb, hk), F32),
            pltpu.VMEM((bb, hv), F32),
        ],
        compiler_params=_params("parallel", "arbitrary"),
        name="delta_sample",
    )(qkv, qkv, qkv, gb, h_main, nw, state, new_states)


def kernel(x_prompt, x_sample, state_pool, state_conv, state_delta, w_in, w_pool_grp, pool_scale, w_pool_up,
           conv_w, a_log, dt_bias, o_norm_w, w_delta_up, w_out, ln1_g, ln1_b, w_gate, w_up, w_down, ln2_g, ln2_b):
    nb, t, d = x_prompt.shape
    bs, ts, _ = x_sample.shape
    depth = w_in.shape[0]
    pb, dp = state_pool.shape[2], state_pool.shape[3]
    cwm1 = state_conv.shape[2]
    assert t >= pb and t >= cwm1
    dqkv = state_conv.shape[3]
    nh, hk, hv = state_delta.shape[2], state_delta.shape[3], state_delta.shape[4]
    dv = nh * hv
    dff = w_gate.shape[-1]
    assert dqkv == 2 * nh * hk + dv and hk == hv
    mp, ms = nb * t, bs * ts
    m = mp + ms
    tb = ms
    assert t % tb == 0 and tb % DELTA_CHUNK == 0
    steps_per_seq = t // tb
    n_prompt_steps = mp // tb
    alpha = float((2 * depth) ** 0.25)

    off_qkv = dp
    off_z = off_qkv + dqkv
    off_beta = off_z + dv
    off_ga = off_beta + 2 * nh
    off_gb = off_ga + d
    assert w_in.shape[-1] == off_gb + d
    col_qkv, col_z, col_pool = 0, dqkv, dqkv + dv
    col_ga = col_pool + dp
    col_gb = col_ga + d
    col_small = col_gb + d
    segments = ((off_qkv, col_qkv, dqkv), (off_z, col_z, dv), (0, col_pool, dp), (off_ga, col_ga, 2 * d),
                (off_beta, col_small, INPROJ_BLOCK_N))
    assert off_beta + INPROJ_BLOCK_N <= w_in.shape[-1]
    w_in_t = jnp.swapaxes(w_in, 1, 2)
    w_grp16 = w_pool_grp.astype(BF16)

    lane_pad = ((0, 0), (nh, V7X_LANES - 2 * nh))
    alog_row = jnp.pad(a_log, lane_pad)[:, None, :]
    dtb_row = jnp.pad(dt_bias, lane_pad)[:, None, :]
    scale_row = pool_scale[:, None, :]
    nw_row = o_norm_w[:, None, :]
    ln1_g, ln1_b, ln2_g, ln2_b = (a[:, None, :] for a in (ln1_g, ln1_b, ln2_g, ln2_b))
    st_pool_tm = jnp.swapaxes(state_pool, 1, 2)
    st_conv_tm = jnp.swapaxes(state_conv, 1, 2)

    x = jnp.concatenate([x_prompt.reshape(mp, d), jnp.swapaxes(x_sample, 0, 1).reshape(ms, d)], axis=0)
    x16 = x.astype(BF16)

    pool_p, conv_p, delta_p, pool_s, conv_s = [], [], [], [], []
    new_delta_s = jnp.zeros(state_delta.shape, F32)
    for l in range(depth):
        h_main = _inproj(x16, w_in_t, segments, l)
        yp = _pool(h_main, st_pool_tm, w_grp16, scale_row, l, col_pool, tb, bs, ts, n_prompt_steps, steps_per_seq)
        qkv, gb = _prep(h_main, st_conv_tm, conv_w, alog_row, dtb_row, l, col_qkv, col_small, tb, bs, ts, nh, hk,
                        n_prompt_steps, steps_per_seq)
        og_p, s_p = _delta_prompt(qkv, gb, h_main, nw_row, l, col_z, nb, t, nh, hk, hv, hg=min(nh, 16))
        og_s, new_delta_s = _delta_sample(qkv, gb, h_main, nw_row, state_delta, new_delta_s, l, col_z, mp, bs, ts,
                                          nh, hk, hv)
        merged = _merge(yp, og_p, og_s, w_pool_up, w_delta_up, h_main, l, col_ga, col_gb, tb)
        r1 = _outproj(merged, w_out, x, l, alpha)
        x1, x1_16 = _layer_norm(r1, ln1_g, ln1_b, l)
        act, w_down16 = _glu(x1_16, w_gate, w_up, w_down, l)
        r2 = _down(act, w_down16, x1, alpha)
        if l + 1 < depth:
            x, x16 = _layer_norm(r2, ln2_g, ln2_b, l)
        else:
            y_p, y_s = _layer_norm_split(r2, ln2_g, ln2_b, l, mp)

        def tail_rows(col, width, keep):
            return jnp.stack([h_main[(b + 1) * t - keep:(b + 1) * t, col:col + width] for b in range(nb)])

        def sample_rows(col, width):
            return jnp.swapaxes(h_main[mp:, col:col + width].reshape(ts, bs, width), 0, 1)

        pool_p.append(tail_rows(col_pool, dp, pb))
        conv_p.append(tail_rows(col_qkv, dqkv, cwm1))
        pool_s.append(jnp.concatenate([state_pool[l], sample_rows(col_pool, dp)], axis=1)[:, -pb:])
        conv_s.append(jnp.concatenate([state_conv[l], sample_rows(col_qkv, dqkv)], axis=1)[:, -cwm1:])
        delta_p.append(s_p)

    y_prompt = y_p.reshape(nb, t, d)
    y_sample = jnp.swapaxes(y_s.reshape(ts, bs, d), 0, 1)
    return (y_prompt, y_sample, jnp.stack(pool_p), jnp.stack(conv_p), jnp.stack(delta_p),
            jnp.stack(pool_s), jnp.stack(conv_s), new_delta_s)
```

```python
import functools
import math

import jax
import jax.numpy as jnp
from jax import lax
from jax.experimental import pallas as pl
from jax.experimental.pallas import tpu as pltpu

F32 = jnp.float32
BF16 = jnp.bfloat16

POOL_WINDOWS = (2, 4, 8, 16)
DELTA_CHUNK = 64
PAST_LEN = 16384
LN_EPS = 1e-5
RMS_EPS = 1e-6

V7X_LANES = 128
V7X_SUBLANES = 8
V7X_BF16_ROWS = 16
V7X_VMEM_LIMIT_BYTES = 56 * 1024 * 1024
POOL_HALO = 16
CONV_HALO = V7X_SUBLANES

ROWS_DOUBLE_BUFFERED = 1088
ROWS_SINGLE_BUFFERED = 2176
ROWS_LAYER_NORM = 272
GLU_BLOCK_N = 256
INPROJ_BLOCK_N = 512
OUT_BLOCK_N = 512
MERGE_BLOCK_N = 1024
ROWS_DOWN = 544
DOWN_BLOCK_K = 11008
SAMPLE_BLOCK_SEQS = 128
SAMPLE_GROUP = 8


def _params(*semantics):
    return pltpu.CompilerParams(dimension_semantics=semantics, vmem_limit_bytes=V7X_VMEM_LIMIT_BYTES)


def _divisor(n, target, mult):
    best = None
    for d in range(mult, min(n, target) + 1, mult):
        if n % d == 0:
            best = d
    assert best is not None, (n, target, mult)
    return best


def _dot(a, b):
    return jnp.dot(a, b, preferred_element_type=F32)


def _dot_t(a, b):
    return lax.dot_general(a, b, (((1,), (1,)), ((), ())), preferred_element_type=F32)


def _tdot(a, b):
    return lax.dot_general(a, b, (((0,), (0,)), ((), ())), preferred_element_type=F32)


def _softplus(x):
    return jnp.maximum(x, 0.0) + jnp.log1p(jnp.exp(-jnp.abs(x)))


def _inproj_body(x_ref, w_ref, o_ref):
    x = x_ref[...]
    o_ref[...] = _dot(x, w_ref[0].astype(x.dtype).T)


def _inproj(x, w_in_t, segments, l):
    M, K = x.shape
    n_out = sum(width for _, _, width in segments)
    bm = _divisor(M, ROWS_SINGLE_BUFFERED, V7X_BF16_ROWS)
    bn = INPROJ_BLOCK_N
    assert all(src % V7X_SUBLANES == 0 and dst % bn == 0 and width % bn == 0 for src, dst, width in segments)

    def src_row(j):
        col = j * bn
        row = 0
        for src, dst, width in segments:
            row = jnp.where((col >= dst) & (col < dst + width), src + col - dst, row)
        return pl.multiple_of(row, V7X_SUBLANES)

    return pl.pallas_call(
        _inproj_body,
        grid=(M // bm, n_out // bn),
        in_specs=[
            pl.BlockSpec((bm, K), lambda i, j: (i, 0), pipeline_mode=pl.Buffered(1)),
            pl.BlockSpec((pl.Element(1), pl.Element(bn), pl.Element(K)), lambda i, j: (l, src_row(j), 0)),
        ],
        out_specs=pl.BlockSpec((bm, bn), lambda i, j: (i, j)),
        out_shape=jax.ShapeDtypeStruct((M, n_out), F32),
        compiler_params=_params("parallel", "arbitrary"),
        name="inproj",
    )(x, w_in_t)


def _merge_body(yp_ref, ogp_ref, ogs_ref, wpu_ref, wdu_ref, ga_ref, gb_ref, o_ref, *, n_prompt_blocks):
    og = jnp.where(pl.program_id(1) < n_prompt_blocks, ogp_ref[...], ogs_ref[...])
    ya = _dot(yp_ref[...], wpu_ref[...].astype(og.dtype))
    yb = _dot(og, wdu_ref[...].astype(og.dtype))
    o_ref[...] = (jax.nn.sigmoid(ga_ref[...]) * ya + jax.nn.sigmoid(gb_ref[...]) * yb).astype(o_ref.dtype)


def _merge(yp, og_p, og_s, w_pool_up, w_delta_up, h_main, l, col_ga, col_gb, tb):
    M, DP = yp.shape
    DV = og_p.shape[1]
    D = w_pool_up.shape[-1]
    bn = _divisor(math.gcd(D, col_ga, col_gb), MERGE_BLOCK_N, V7X_LANES)
    npb = og_p.shape[0] // tb
    assert og_s.shape[0] == tb and col_ga % bn == 0 and col_gb % bn == 0
    return pl.pallas_call(
        functools.partial(_merge_body, n_prompt_blocks=npb),
        grid=(D // bn, M // tb),
        in_specs=[
            pl.BlockSpec((tb, DP), lambda j, i: (i, 0)),
            pl.BlockSpec((tb, DV), lambda j, i: (jnp.minimum(i, npb - 1), 0)),
            pl.BlockSpec((tb, DV), lambda j, i: (0, 0), pipeline_mode=pl.Buffered(1)),
            pl.BlockSpec((None, DP, bn), lambda j, i: (l, 0, j), pipeline_mode=pl.Buffered(1)),
            pl.BlockSpec((None, DV, bn), lambda j, i: (l, 0, j), pipeline_mode=pl.Buffered(1)),
            pl.BlockSpec((tb, bn), lambda j, i: (i, col_ga // bn + j)),
            pl.BlockSpec((tb, bn), lambda j, i: (i, col_gb // bn + j)),
        ],
        out_specs=pl.BlockSpec((tb, bn), lambda j, i: (i, j)),
        out_shape=jax.ShapeDtypeStruct((M, D), BF16),
        compiler_params=_params("parallel", "parallel"),
        name="merge",
    )(yp, og_p, og_s, w_pool_up, w_delta_up, h_main, h_main)


def _outproj_body(a_ref, w_ref, x_ref, o_ref, *, alpha):
    a = a_ref[...]
    o_ref[...] = alpha * x_ref[...] + _dot(a, w_ref[...].astype(a.dtype))


def _outproj(a, w, x, l, alpha):
    M, K = a.shape
    N = w.shape[-1]
    bm = _divisor(M, ROWS_DOUBLE_BUFFERED, V7X_BF16_ROWS)
    bn = _divisor(N, OUT_BLOCK_N, V7X_LANES)
    return pl.pallas_call(
        functools.partial(_outproj_body, alpha=alpha),
        grid=(M // bm, N // bn),
        in_specs=[
            pl.BlockSpec((bm, K), lambda i, j: (i, 0)),
            pl.BlockSpec((None, K, bn), lambda i, j: (l, 0, j)),
            pl.BlockSpec((bm, bn), lambda i, j: (i, j)),
        ],
        out_specs=pl.BlockSpec((bm, bn), lambda i, j: (i, j)),
        out_shape=jax.ShapeDtypeStruct((M, N), F32),
        compiler_params=_params("parallel", "parallel"),
        name="outproj",
    )(a, w, x)


def _glu_body(x_ref, wg_ref, wu_ref, wd_ref, o_ref, wd16_ref, *, n_cast_steps):
    x = x_ref[...]
    g = _dot(x, wg_ref[...].astype(x.dtype))
    u = _dot(x, wu_ref[...].astype(x.dtype))
    o_ref[...] = (g * jax.nn.sigmoid(g) * u).astype(o_ref.dtype)

    @pl.when(pl.program_id(0) * pl.num_programs(1) + pl.program_id(1) < n_cast_steps)
    def _():
        wd16_ref[...] = wd_ref[...].astype(wd16_ref.dtype)


def _glu(x, w_gate, w_up, w_down, l):
    M, K = x.shape
    N = w_gate.shape[-1]
    F, D = w_down.shape[1], w_down.shape[2]
    bm = _divisor(M, ROWS_SINGLE_BUFFERED, V7X_BF16_ROWS)
    bn = _divisor(N, GLU_BLOCK_N, V7X_LANES)
    nj = N // bn
    steps = (M // bm) * nj
    slab = min(s for s in range(V7X_BF16_ROWS, F + 1, V7X_BF16_ROWS) if F % s == 0 and s >= 128 and F // s <= steps)
    n_cast_steps = F // slab

    def slab_index(i, j):
        return jnp.minimum(i * nj + j, n_cast_steps - 1)

    return pl.pallas_call(
        functools.partial(_glu_body, n_cast_steps=n_cast_steps),
        grid=(M // bm, nj),
        in_specs=[
            pl.BlockSpec((bm, K), lambda i, j: (i, 0), pipeline_mode=pl.Buffered(1)),
            pl.BlockSpec((None, K, bn), lambda i, j: (l, 0, j)),
            pl.BlockSpec((None, K, bn), lambda i, j: (l, 0, j)),
            pl.BlockSpec((None, slab, D), lambda i, j: (l, slab_index(i, j), 0)),
        ],
        out_specs=[
            pl.BlockSpec((bm, bn), lambda i, j: (i, j)),
            pl.BlockSpec((slab, D), lambda i, j: (slab_index(i, j), 0)),
        ],
        out_shape=[jax.ShapeDtypeStruct((M, N), BF16), jax.ShapeDtypeStruct((F, D), BF16)],
        compiler_params=_params("arbitrary", "arbitrary"),
        name="glu",
    )(x, w_gate, w_up, w_down)


def _down_body(a_ref, w_ref, x_ref, o_ref, *, alpha):
    @pl.when(pl.program_id(2) == 0)
    def _():
        o_ref[...] = alpha * x_ref[...]

    o_ref[...] += _dot(a_ref[...], w_ref[...])


def _down(a, w, x, alpha):
    M, K = a.shape
    N = w.shape[-1]
    bm = _divisor(M, ROWS_DOWN, V7X_BF16_ROWS)
    bn = _divisor(N, OUT_BLOCK_N, V7X_LANES)
    tk = _divisor(K, DOWN_BLOCK_K, V7X_LANES)
    return pl.pallas_call(
        functools.partial(_down_body, alpha=alpha),
        grid=(M // bm, N // bn, K // tk),
        in_specs=[
            pl.BlockSpec((bm, tk), lambda i, j, k: (i, k)),
            pl.BlockSpec((tk, bn), lambda i, j, k: (k, j)),
            pl.BlockSpec((bm, bn), lambda i, j, k: (i, j)),
        ],
        out_specs=pl.BlockSpec((bm, bn), lambda i, j, k: (i, j)),
        out_shape=jax.ShapeDtypeStruct((M, N), F32),
        compiler_params=_params("parallel", "parallel", "arbitrary"),
        name="down",
    )(a, w, x)


def _ln_body(r_ref, g_ref, b_ref, xf_ref, xb_ref):
    r = r_ref[...]
    mu = jnp.mean(r, axis=-1, keepdims=True)
    xc = r - mu
    var = jnp.mean(xc * xc, axis=-1, keepdims=True)
    y = xc * lax.rsqrt(var + LN_EPS) * g_ref[...] + b_ref[...]
    xf_ref[...] = y
    xb_ref[...] = y.astype(xb_ref.dtype)


def _ln_split_body(r_ref, g_ref, b_ref, yp_ref, ys_ref, *, n_prompt_blocks):
    r = r_ref[...]
    mu = jnp.mean(r, axis=-1, keepdims=True)
    xc = r - mu
    var = jnp.mean(xc * xc, axis=-1, keepdims=True)
    y = xc * lax.rsqrt(var + LN_EPS) * g_ref[...] + b_ref[...]

    @pl.when(pl.program_id(0) < n_prompt_blocks)
    def _():
        yp_ref[...] = y

    @pl.when(pl.program_id(0) >= n_prompt_blocks)
    def _():
        ys_ref[...] = y


def _layer_norm_split(r, g, b, l, mp):
    M, D = r.shape
    ms = M - mp
    bm = _divisor(math.gcd(mp, ms), ROWS_LAYER_NORM, V7X_SUBLANES)
    npb = mp // bm
    return pl.pallas_call(
        functools.partial(_ln_split_body, n_prompt_blocks=npb),
        grid=(M // bm,),
        in_specs=[
            pl.BlockSpec((bm, D), lambda i: (i, 0)),
            pl.BlockSpec((None, 1, D), lambda i: (l, 0, 0)),
            pl.BlockSpec((None, 1, D), lambda i: (l, 0, 0)),
        ],
        out_specs=[
            pl.BlockSpec((bm, D), lambda i: (jnp.minimum(i, npb - 1), 0)),
            pl.BlockSpec((bm, D), lambda i: (jnp.maximum(i - npb, 0), 0)),
        ],
        out_shape=[jax.ShapeDtypeStruct((mp, D), F32), jax.ShapeDtypeStruct((ms, D), F32)],
        compiler_params=_params("arbitrary"),
        name="layer_norm_split",
    )(r, g, b)


def _layer_norm(r, g, b, l):
    M, D = r.shape
    bm = _divisor(M, ROWS_LAYER_NORM, V7X_BF16_ROWS)
    return pl.pallas_call(
        _ln_body,
        grid=(M // bm,),
        in_specs=[
            pl.BlockSpec((bm, D), lambda i: (i, 0)),
            pl.BlockSpec((None, 1, D), lambda i: (l, 0, 0)),
            pl.BlockSpec((None, 1, D), lambda i: (l, 0, 0)),
        ],
        out_specs=[pl.BlockSpec((bm, D), lambda i: (i, 0)), pl.BlockSpec((bm, D), lambda i: (i, 0))],
        out_shape=[jax.ShapeDtypeStruct((M, D), F32), jax.ShapeDtypeStruct((M, D), BF16)],
        compiler_params=_params("parallel"),
        name="layer_norm",
    )(r, g, b)


def _pool_body(u_ref, st_ref, wg_ref, sc_ref, o_ref, ext_ref, *, tb, bs, ts, pb, gw, n_prompt_steps, steps_per_seq):
    step = pl.program_id(0)

    def finish(mixed):
        for gi, m in enumerate(mixed):
            cols = slice(gi * gw, (gi + 1) * gw)
            y = _dot(m.astype(BF16), wg_ref[gi])
            o_ref[:, cols] = (y * sc_ref[:, cols]).astype(o_ref.dtype)

    @pl.when(step < n_prompt_steps)
    def _():
        blk = step % steps_per_seq

        @pl.when(blk == 0)
        def _():
            ext_ref[0:POOL_HALO, :] = jnp.zeros((POOL_HALO, ext_ref.shape[1]), F32)

        ext_ref[POOL_HALO:POOL_HALO + tb, :] = u_ref[...]
        pos = blk * tb + lax.broadcasted_iota(jnp.int32, (tb, 1), 0)
        mixed = []
        for gi, w in enumerate(POOL_WINDOWS):
            cols = slice(gi * gw, (gi + 1) * gw)
            cur = ext_ref[POOL_HALO:POOL_HALO + tb, cols]
            win = cur
            for i in range(1, w):
                win = win + ext_ref[POOL_HALO - i:POOL_HALO - i + tb, cols]
            cnt = jnp.minimum(pos + 1, w).astype(F32)
            mixed.append(win / cnt - cur)
        finish(mixed)
        ext_ref[0:POOL_HALO, :] = ext_ref[tb:tb + POOL_HALO, :]

    @pl.when(step >= n_prompt_steps)
    def _():
        mixed = []
        for gi, w in enumerate(POOL_WINDOWS):
            cols = slice(gi * gw, (gi + 1) * gw)
            per_t = []
            for t in range(ts):
                acc = None
                for i in range(w):
                    e = pb + t - i
                    term = st_ref[e, :, cols] if e < pb else u_ref[(e - pb) * bs:(e - pb + 1) * bs, cols]
                    acc = term if acc is None else acc + term
                cnt = float(min(PAST_LEN + t + 1, w))
                per_t.append(acc / cnt - u_ref[t * bs:(t + 1) * bs, cols])
            mixed.append(jnp.concatenate(per_t, axis=0))
        finish(mixed)


def _pool(h_main, st_pool_tm, w_grp, scale, l, col_pool, tb, bs, ts, n_prompt_steps, steps_per_seq):
    M = h_main.shape[0]
    _, pb, _, DP = st_pool_tm.shape
    G, gw = w_grp.shape[1], w_grp.shape[2]
    assert G == len(POOL_WINDOWS) and pb == max(POOL_WINDOWS) - 1 and pb <= POOL_HALO and col_pool % DP == 0
    body = functools.partial(_pool_body, tb=tb, bs=bs, ts=ts, pb=pb, gw=gw,
                             n_prompt_steps=n_prompt_steps, steps_per_seq=steps_per_seq)
    return pl.pallas_call(
        body,
        grid=(M // tb,),
        in_specs=[
            pl.BlockSpec((tb, DP), lambda s: (s, col_pool // DP)),
            pl.BlockSpec((None, pb, bs, DP), lambda s: (l, 0, 0, 0)),
            pl.BlockSpec((None, G, gw, gw), lambda s: (l, 0, 0, 0)),
            pl.BlockSpec((None, 1, DP), lambda s: (l, 0, 0)),
        ],
        out_specs=pl.BlockSpec((tb, DP), lambda s: (s, 0)),
        out_shape=jax.ShapeDtypeStruct((M, DP), BF16),
        scratch_shapes=[pltpu.VMEM((POOL_HALO + tb, DP), F32)],
        compiler_params=_params("arbitrary"),
        name="pool_mixer",
    )(h_main, st_pool_tm, w_grp, scale)


def _prep_body(x_ref, hs_ref, stc_ref, cw_ref, al_ref, dtb_ref, qkv_ref, gb_ref, ext_ref, halo_ref, *,
               tb, bs, ts, cw, nh, hk, chunk, n_prompt_steps, steps_per_seq):
    step = pl.program_id(0)
    sec = pl.program_id(1)
    half = V7X_LANES // 2

    def emit(conv_head):
        @pl.when(sec < 2)
        def _():
            qscale = jnp.where(sec == 0, hk ** -0.5, 1.0).astype(F32)
            for h in range(nh):
                y = conv_head(h)
                y = y * jax.nn.sigmoid(y)
                ss = jnp.sum(y * y, axis=-1, keepdims=True)
                qkv_ref[h] = y * (lax.rsqrt(ss + RMS_EPS) * qscale)

        @pl.when(sec == 2)
        def _():
            for h in range(nh):
                y = conv_head(h)
                qkv_ref[h] = y * jax.nn.sigmoid(y)

    def emit_gates(gcum, beta):
        for h in range(nh):
            gcol = jnp.broadcast_to(gcum[:, nh + h:nh + h + 1], (tb, half))
            bcol = jnp.broadcast_to(beta[:, h:h + 1], (tb, half))
            gb_ref[h] = jnp.concatenate([gcol, bcol], axis=1)

    def gate_values():
        hs = hs_ref[...]
        beta = jax.nn.sigmoid(hs)
        g = -jnp.exp(al_ref[...]) * _softplus(hs + dtb_ref[...])
        return g, beta

    @pl.when(step < n_prompt_steps)
    def _():
        blk = step % steps_per_seq

        @pl.when(blk == 0)
        def _():
            ext_ref[0:CONV_HALO, :] = jnp.zeros((CONV_HALO, ext_ref.shape[1]), F32)

        @pl.when(blk != 0)
        def _():
            ext_ref[0:CONV_HALO, :] = halo_ref[sec]

        ext_ref[CONV_HALO:CONV_HALO + tb, :] = x_ref[...]
        halo_ref[sec] = ext_ref[tb:tb + CONV_HALO, :]
        base = CONV_HALO - (cw - 1)

        def conv_head(h):
            cols = slice(h * hk, (h + 1) * hk)
            y = ext_ref[base:base + tb, cols] * cw_ref[0:1, cols]
            for i in range(1, cw):
                y = y + ext_ref[base + i:base + i + tb, cols] * cw_ref[i:i + 1, cols]
            return y

        emit(conv_head)

        @pl.when(sec == 0)
        def _():
            g, beta = gate_values()
            t_in = lax.broadcasted_iota(jnp.int32, g.shape, 0) % chunk
            d = 1
            while d < chunk:
                g = g + jnp.where(t_in >= d, pltpu.roll(g, d, 0), 0.0)
                d *= 2
            emit_gates(g, beta)

    @pl.when(step >= n_prompt_steps)
    def _():
        def ext_slab(j, cols):
            return stc_ref[j, :, cols] if j < cw - 1 else x_ref[(j - (cw - 1)) * bs:(j - (cw - 2)) * bs, cols]

        def conv_head(h):
            cols = slice(h * hk, (h + 1) * hk)
            per_t = []
            for t in range(ts):
                y = ext_slab(t, cols) * cw_ref[0:1, cols]
                for i in range(1, cw):
                    y = y + ext_slab(t + i, cols) * cw_ref[i:i + 1, cols]
                per_t.append(y)
            return jnp.concatenate(per_t, axis=0)

        emit(conv_head)

        @pl.when(sec == 0)
        def _():
            g, beta = gate_values()
            slabs = [g[0:bs]]
            for t in range(1, ts):
                slabs.append(slabs[-1] + g[t * bs:(t + 1) * bs])
            emit_gates(jnp.concatenate(slabs, axis=0), beta)


def _prep(h_main, st_conv_tm, conv_w, alog_row, dtb_row, l, col_qkv, col_small, tb, bs, ts, nh, hk,
          n_prompt_steps, steps_per_seq):
    M = h_main.shape[0]
    cw = conv_w.shape[1]
    sw = nh * hk
    assert cw - 1 <= CONV_HALO and col_qkv % sw == 0 and 2 * nh <= V7X_LANES and col_small % V7X_LANES == 0
    body = functools.partial(_prep_body, tb=tb, bs=bs, ts=ts, cw=cw, nh=nh, hk=hk, chunk=DELTA_CHUNK,
                             n_prompt_steps=n_prompt_steps, steps_per_seq=steps_per_seq)
    return pl.pallas_call(
        body,
        grid=(M // tb, 3),
        in_specs=[
            pl.BlockSpec((tb, sw), lambda s, c: (s, col_qkv // sw + c)),
            pl.BlockSpec((tb, V7X_LANES), lambda s, c: (s, col_small // V7X_LANES)),
            pl.BlockSpec((None, cw - 1, bs, sw), lambda s, c: (l, 0, 0, c)),
            pl.BlockSpec((None, cw, sw), lambda s, c: (l, 0, c)),
            pl.BlockSpec((None, 1, V7X_LANES), lambda s, c: (l, 0, 0)),
            pl.BlockSpec((None, 1, V7X_LANES), lambda s, c: (l, 0, 0)),
        ],
        out_specs=[
            pl.BlockSpec((nh, tb, hk), lambda s, c: (c, s, 0)),
            pl.BlockSpec((nh, tb, V7X_LANES), lambda s, c: (0, s, 0)),
        ],
        out_shape=[jax.ShapeDtypeStruct((3 * nh, M, hk), F32), jax.ShapeDtypeStruct((nh, M, V7X_LANES), F32)],
        scratch_shapes=[pltpu.VMEM((CONV_HALO + tb, sw), F32), pltpu.VMEM((3, CONV_HALO, sw), F32)],
        compiler_params=_params("arbitrary", "arbitrary"),
        name="delta_prep",
    )(h_main, h_main, st_conv_tm, conv_w, alog_row, dtb_row)


def _gated_norm(o, z, nw):
    o = o * lax.rsqrt(jnp.mean(o * o, axis=-1, keepdims=True) + RMS_EPS) * nw
    return o * (z * jax.nn.sigmoid(z))


def _delta_prompt_body(q_ref, k_ref, v_ref, gb_ref, z_ref, nw_ref, og_ref, so_ref, s_ref, *, hg, c, hk, hv):
    ci = pl.program_id(2)
    half = V7X_LANES // 2

    @pl.when(ci == 0)
    def _():
        s_ref[...] = jnp.zeros(s_ref.shape, F32)

    ti = lax.broadcasted_iota(jnp.int32, (c, c), 0)
    si = lax.broadcasted_iota(jnp.int32, (c, c), 1)
    incl = ti >= si
    strict = ti > si
    heads = range(hg)
    q = [q_ref[j] for j in heads]
    k = [k_ref[j] for j in heads]
    gbv = [gb_ref[j] for j in heads]
    g_col = [x[:, 0:1] for x in gbv]
    b_col = [x[:, half:half + 1] for x in gbv]
    decay = [jnp.where(incl, jnp.exp(jnp.where(incl, x[:, 0:c] - jnp.concatenate([x, x], axis=0).T[0:c, 0:c], 0.0)), 0.0)
             for x in gbv]
    kb = [k[j] * b_col[j] for j in heads]
    k16 = [x.astype(BF16) for x in k]
    a = [jnp.where(strict, _dot_t(kb[j].astype(BF16), k16[j]) * decay[j], 0.0) for j in heads]
    qk = [_dot_t(q[j].astype(BF16), k16[j]) * decay[j] for j in heads]
    n = [-x for x in a]
    p = [_dot(x.astype(BF16), x.astype(BF16)) for x in a]
    span = 4
    while span < c:
        both = [_dot(jnp.concatenate([n[j], p[j]], axis=0).astype(BF16), p[j].astype(BF16)) for j in heads]
        n = [n[j] + p[j] + both[j][0:c] for j in heads]
        p = [both[j][c:2 * c] for j in heads]
        span *= 2
    n = [n[j] + p[j] + _dot(n[j].astype(BF16), p[j].astype(BF16)) for j in heads]
    eg = [jnp.exp(x) for x in g_col]
    rhs = [jnp.concatenate([kb[j] * eg[j], v_ref[j] * b_col[j]], axis=1) for j in heads]
    tr = [rhs[j] + _dot(n[j].astype(BF16), rhs[j].astype(BF16)) for j in heads]
    g_last = [x[c - 1:c, :] for x in g_col]
    k_tail = [(k[j] * jnp.exp(g_last[j] - g_col[j])).astype(BF16) for j in heads]
    s0 = [s_ref[j] for j in heads]
    ps = [_dot(jnp.concatenate([tr[j][:, 0:hk], q[j] * eg[j]], axis=0).astype(BF16), s0[j].astype(BF16)) for j in heads]
    u = [(tr[j][:, hk:hk + hv] - ps[j][0:c]).astype(BF16) for j in heads]
    for j in heads:
        s_ref[j] = jnp.exp(g_last[j]) * s0[j] + _tdot(k_tail[j], u[j])
    o = [ps[j][c:2 * c] + _dot(qk[j].astype(BF16), u[j]) for j in heads]
    for j in heads:
        cols = slice(j * hv, (j + 1) * hv)
        og_ref[:, cols] = _gated_norm(o[j], z_ref[:, cols], nw_ref[...]).astype(og_ref.dtype)

    @pl.when(ci == pl.num_programs(2) - 1)
    def _():
        so_ref[...] = s_ref[...]


def _delta_prompt(qkv, gb, h_main, nw, l, col_z, nb, t, nh, hk, hv, hg):
    c = DELTA_CHUNK
    assert t % c == 0 and nh % hg == 0 and col_z % (hg * hv) == 0 and hk == V7X_LANES and c <= V7X_LANES // 2
    ncs = t // c
    ngr = nh // hg
    body = functools.partial(_delta_prompt_body, hg=hg, c=c, hk=hk, hv=hv)

    def head_spec(sec):
        return pl.BlockSpec((hg, c, hk), lambda b, g, ci: (sec * ngr + g, b * ncs + ci, 0))

    return pl.pallas_call(
        body,
        grid=(nb, ngr, ncs),
        in_specs=[
            head_spec(0), head_spec(1), head_spec(2),
            pl.BlockSpec((hg, c, V7X_LANES), lambda b, g, ci: (g, b * ncs + ci, 0)),
            pl.BlockSpec((c, hg * hv), lambda b, g, ci: (b * ncs + ci, col_z // (hg * hv) + g)),
            pl.BlockSpec((None, 1, hv), lambda b, g, ci: (l, 0, 0)),
        ],
        out_specs=[
            pl.BlockSpec((c, hg * hv), lambda b, g, ci: (b * ncs + ci, g)),
            pl.BlockSpec((None, hg, hk, hv), lambda b, g, ci: (b, g, 0, 0)),
        ],
        out_shape=[jax.ShapeDtypeStruct((nb * t, nh * hv), BF16), jax.ShapeDtypeStruct((nb, nh, hk, hv), F32)],
        scratch_shapes=[pltpu.VMEM((hg, hk, hv), F32)],
        compiler_params=_params("parallel", "parallel", "arbitrary"),
        name="delta_prompt",
    )(qkv, qkv, qkv, gb, h_main, nw)


def _delta_sample_body(q_ref, k_ref, v_ref, gb_ref, z_ref, nw_ref, s_ref, all_states_ref, og_ref, so_ref,
                       kq_scr, p_scr, u_scr, kt_scr, e_scr, *, bs, ts, bb, hk, hv):
    del all_states_ref
    b0 = pl.multiple_of(pl.program_id(1) * bb, bb)
    half = V7X_LANES // 2
    pad = V7X_SUBLANES - ts

    def rows(ref, t):
        return ref[pl.ds(t * bs + b0, bb), :]

    ks = [rows(k_ref, t) for t in range(ts)]
    qs = [rows(q_ref, t) for t in range(ts)]
    for t in range(ts):
        kq_scr[t * bb:(t + 1) * bb, :] = ks[t]
        kq_scr[(ts + t) * bb:(ts + t + 1) * bb, :] = qs[t]

    def state_products(g, carry):
        seqs = [g * SAMPLE_GROUP + i for i in range(SAMPLE_GROUP)]
        kq = [kq_scr[pl.ds(b, 2 * ts, stride=bb), :].astype(BF16) for b in seqs]
        s16 = [s_ref[b].astype(BF16) for b in seqs]
        prod = [_dot(kq[i], s16[i]) for i in range(SAMPLE_GROUP)]
        for i, b in enumerate(seqs):
            p_scr[pl.ds(b, 2 * ts, stride=bb), :] = prod[i]
        return carry

    lax.fori_loop(0, bb // SAMPLE_GROUP, state_products, 0)

    gs = [rows(gb_ref, t)[:, 0:1] for t in range(ts)]
    betas = [rows(gb_ref, t)[:, half:half + 1] for t in range(ts)]
    us = []
    for t in range(ts):
        u = betas[t] * (rows(v_ref, t) - jnp.exp(gs[t]) * p_scr[t * bb:(t + 1) * bb, :])
        for s in range(t):
            kk = jnp.sum(ks[t] * ks[s], axis=-1, keepdims=True)
            u = u - (betas[t] * kk * jnp.exp(gs[t] - gs[s])) * us[s]
        us.append(u)
    for t in range(ts):
        o = jnp.exp(gs[t]) * p_scr[(ts + t) * bb:(ts + t + 1) * bb, :]
        for s in range(t + 1):
            qk = jnp.sum(qs[t] * ks[s], axis=-1, keepdims=True)
            o = o + (qk * jnp.exp(gs[t] - gs[s])) * us[s]
        og_ref[pl.ds(t * bs + b0, bb), :] = _gated_norm(o, rows(z_ref, t), nw_ref[...]).astype(og_ref.dtype)
        u_scr[t * bb:(t + 1) * bb, :] = us[t]
        kt_scr[t * bb:(t + 1) * bb, :] = ks[t] * jnp.exp(gs[ts - 1] - gs[t])
    u_scr[ts * bb:(ts + pad) * bb, :] = jnp.zeros((pad * bb, hv), F32)
    kt_scr[ts * bb:(ts + pad) * bb, :] = jnp.zeros((pad * bb, hk), F32)
    e_scr[...] = jnp.broadcast_to(jnp.exp(gs[ts - 1]), (bb, hv))

    def state_update(g, carry):
        seqs = [g * SAMPLE_GROUP + i for i in range(SAMPLE_GROUP)]
        ub = [u_scr[pl.ds(b, V7X_SUBLANES, stride=bb), :].astype(BF16) for b in seqs]
        kb = [kt_scr[pl.ds(b, V7X_SUBLANES, stride=bb), :].astype(BF16) for b in seqs]
        upd = [_tdot(kb[i], ub[i]) for i in range(SAMPLE_GROUP)]
        for i, b in enumerate(seqs):
            so_ref[b] = e_scr[pl.ds(b, 1), :] * s_ref[b] + upd[i]
        return carry

    lax.fori_loop(0, bb // SAMPLE_GROUP, state_update, 0)


def _delta_sample(qkv, gb, h_main, nw, state, new_states, l, col_z, row0, bs, ts, nh, hk, hv):
    tbs = bs * ts
    bb = _divisor(bs, SAMPLE_BLOCK_SEQS, V7X_BF16_ROWS)
    assert row0 % tbs == 0 and col_z % hv == 0 and 2 * ts <= V7X_SUBLANES and bb % SAMPLE_GROUP == 0
    body = functools.partial(_delta_sample_body, bs=bs, ts=ts, bb=bb, hk=hk, hv=hv)

    def head_spec(sec):
        return pl.BlockSpec((None, tbs, hk), lambda h, g: (sec * nh + h, row0 // tbs, 0))

    return pl.pallas_call(
        body,
        grid=(nh, bs // bb),
        in_specs=[
            head_spec(0), head_spec(1), head_spec(2),
            pl.BlockSpec((None, tbs, V7X_LANES), lambda h, g: (h, row0 // tbs, 0)),
            pl.BlockSpec((tbs, hv), lambda h, g: (row0 // tbs, col_z // hv + h)),
            pl.BlockSpec((None, 1, hv), lambda h, g: (l, 0, 0)),
            pl.BlockSpec((None, bb, None, hk, hv), lambda h, g: (l, g, h, 0, 0)),
            pl.BlockSpec(memory_space=pl.ANY),
        ],
        out_specs=[
            pl.BlockSpec((tbs, hv), lambda h, g: (0, h)),
            pl.BlockSpec((None, bb, None, hk, hv), lambda h, g: (l, g, h, 0, 0)),
        ],
        out_shape=[jax.ShapeDtypeStruct((tbs, nh * hv), BF16), jax.ShapeDtypeStruct(new_states.shape, F32)],
        input_output_aliases={7: 1},
        scratch_shapes=[
            pltpu.VMEM((2 * ts * bb, hk), F32),
            pltpu.VMEM((2 * ts * bb, hv), F32),
            pltpu.VMEM((V7X_SUBLANES * bb, hv), F32),
            pltpu.VMEM((V7X_SUBLANES * bb, hk), F32),
            pltpu.VMEM((bb, hv), F32),
        ],
        compiler_params=_params("parallel", "arbitrary"),
        name="delta_sample",
    )(qkv, qkv, qkv, gb, h_main, nw, state, new_states)


def kernel(x_prompt, x_sample, state_pool, state_conv, state_delta, w_in, w_pool_grp, pool_scale, w_pool_up,
           conv_w, a_log, dt_bias, o_norm_w, w_delta_up, w_out, ln1_g, ln1_b, w_gate, w_up, w_down, ln2_g, ln2_b):
    nb, t, d = x_prompt.shape
    bs, ts, _ = x_sample.shape
    depth = w_in.shape[0]
    pb, dp = state_pool.shape[2], state_pool.shape[3]
    cwm1 = state_conv.shape[2]
    assert t >= pb and t >= cwm1
    dqkv = state_conv.shape[3]
    nh, hk, hv = state_delta.shape[2], state_delta.shape[3], state_delta.shape[4]
    dv = nh * hv
    dff = w_gate.shape[-1]
    assert dqkv == 2 * nh * hk + dv and hk == hv
    mp, ms = nb * t, bs * ts
    m = mp + ms
    tb = ms
    assert t % tb == 0 and tb % DELTA_CHUNK == 0
    steps_per_seq = t // tb
    n_prompt_steps = mp // tb
    alpha = float((2 * depth) ** 0.25)

    off_qkv = dp
    off_z = off_qkv + dqkv
    off_beta = off_z + dv
    off_ga = off_beta + 2 * nh
    off_gb = off_ga + d
    assert w_in.shape[-1] == off_gb + d
    col_qkv, col_z, col_pool = 0, dqkv, dqkv + dv
    col_ga = col_pool + dp
    col_gb = col_ga + d
    col_small = col_gb + d
    segments = ((off_qkv, col_qkv, dqkv), (off_z, col_z, dv), (0, col_pool, dp), (off_ga, col_ga, 2 * d),
                (off_beta, col_small, INPROJ_BLOCK_N))
    assert off_beta + INPROJ_BLOCK_N <= w_in.shape[-1]
    w_in_t = jnp.swapaxes(w_in, 1, 2)
    w_grp16 = w_pool_grp.astype(BF16)

    lane_pad = ((0, 0), (nh, V7X_LANES - 2 * nh))
    alog_row = jnp.pad(a_log, lane_pad)[:, None, :]
    dtb_row = jnp.pad(dt_bias, lane_pad)[:, None, :]
    scale_row = pool_scale[:, None, :]
    nw_row = o_norm_w[:, None, :]
    ln1_g, ln1_b, ln2_g, ln2_b = (a[:, None, :] for a in (ln1_g, ln1_b, ln2_g, ln2_b))
    st_pool_tm = jnp.swapaxes(state_pool, 1, 2)
    st_conv_tm = jnp.swapaxes(state_conv, 1, 2)

    x = jnp.concatenate([x_prompt.reshape(mp, d), jnp.swapaxes(x_sample, 0, 1).reshape(ms, d)], axis=0)
    x16 = x.astype(BF16)

    pool_p, conv_p, delta_p, pool_s, conv_s = [], [], [], [], []
    new_delta_s = jnp.zeros(state_delta.shape, F32)
    for l in range(depth):
        h_main = _inproj(x16, w_in_t, segments, l)
        yp = _pool(h_main, st_pool_tm, w_grp16, scale_row, l, col_pool, tb, bs, ts, n_prompt_steps, steps_per_seq)
        qkv, gb = _prep(h_main, st_conv_tm, conv_w, alog_row, dtb_row, l, col_qkv, col_small, tb, bs, ts, nh, hk,
                        n_prompt_steps, steps_per_seq)
        og_p, s_p = _delta_prompt(qkv, gb, h_main, nw_row, l, col_z, nb, t, nh, hk, hv, hg=min(nh, 16))
        og_s, new_delta_s = _delta_sample(qkv, gb, h_main, nw_row, state_delta, new_delta_s, l, col_z, mp, bs, ts,
                                          nh, hk, hv)
        merged = _merge(yp, og_p, og_s, w_pool_up, w_delta_up, h_main, l, col_ga, col_gb, tb)
        r1 = _outproj(merged, w_out, x, l, alpha)
        x1, x1_16 = _layer_norm(r1, ln1_g, ln1_b, l)
        act, w_down16 = _glu(x1_16, w_gate, w_up, w_down, l)
        r2 = _down(act, w_down16, x1, alpha)
        if l + 1 < depth:
            x, x16 = _layer_norm(r2, ln2_g, ln2_b, l)
        else:
            y_p, y_s = _layer_norm_split(r2, ln2_g, ln2_b, l, mp)

        def tail_rows(col, width, keep):
            return jnp.stack([h_main[(b + 1) * t - keep:(b + 1) * t, col:col + width] for b in range(nb)])

        def sample_rows(col, width):
            return jnp.swapaxes(h_main[mp:, col:col + width].reshape(ts, bs, width), 0, 1)

        pool_p.append(tail_rows(col_pool, dp, pb))
        conv_p.append(tail_rows(col_qkv, dqkv, cwm1))
        pool_s.append(jnp.concatenate([state_pool[l], sample_rows(col_pool, dp)], axis=1)[:, -pb:])
        conv_s.append(jnp.concatenate([state_conv[l], sample_rows(col_qkv, dqkv)], axis=1)[:, -cwm1:])
        delta_p.append(s_p)

    y_prompt = y_p.reshape(nb, t, d)
    y_sample = jnp.swapaxes(y_s.reshape(ts, bs, d), 0, 1)
    return (y_prompt, y_sample, jnp.stack(pool_p), jnp.stack(conv_p), jnp.stack(delta_p),
            jnp.stack(pool_s), jnp.stack(conv_s), new_delta_s)
```
